```python
import math
import jax, jax.numpy as jnp
from jax import lax
import numpy as np

D_MODEL = 2048
BATCH = 8
SEQ = 2048
DEPTH = 1

GLA_HEADS = 4
GLA_DK = D_MODEL // 16
GLA_DV = D_MODEL // 8
GLA_KEY_WIDTH = GLA_HEADS * GLA_DK
GLA_VAL_WIDTH = GLA_HEADS * GLA_DV
GLA_GATE_RANK = 16
GLA_TAU = 16.0
GLA_CHUNK = 64
CONV_WIDTH = D_MODEL // 2
CONV_KERNEL = 31
MIX_WIDTH = GLA_VAL_WIDTH + CONV_WIDTH
IN_WIDTH = 2 * GLA_KEY_WIDTH + 2 * GLA_VAL_WIDTH + GLA_GATE_RANK + 2 * CONV_WIDTH
SPLIT_Q = GLA_KEY_WIDTH
SPLIT_K = 2 * GLA_KEY_WIDTH
SPLIT_V = 2 * GLA_KEY_WIDTH + GLA_VAL_WIDTH
SPLIT_R = 2 * GLA_KEY_WIDTH + 2 * GLA_VAL_WIDTH
SPLIT_A = 2 * GLA_KEY_WIDTH + 2 * GLA_VAL_WIDTH + GLA_GATE_RANK
PEER_HEADS = 8
PEER_NKEYS = 128
PEER_EXPERTS = PEER_NKEYS * PEER_NKEYS
PEER_QDIM = 256
PEER_TOPK = 16
PEER_BLOCK = 128
LN_EPS = 1e-5
DEEPNORM_ALPHA = (2.0 * DEPTH) ** 0.25
DEEPNORM_BETA = (8.0 * DEPTH) ** -0.25

kernel_name = "hymba_gla_conformer_peer_deepnorm"


def layer_norm(x, g, b=None):
    xf = x.astype(jnp.float32)
    mu = jnp.mean(xf, axis=-1, keepdims=True)
    var = jnp.mean(jnp.square(xf - mu), axis=-1, keepdims=True)
    out = (xf - mu) * lax.rsqrt(var + LN_EPS) * g.astype(jnp.float32)
    if b is not None:
        out = out + b.astype(jnp.float32)
    return out


def gla_chunked(q, k, v, log_a):
    B, L = q.shape[0], q.shape[1]
    C = GLA_CHUNK
    N = L // C

    def to_chunks(t):
        return t.reshape(B, N, C, GLA_HEADS, t.shape[-1]).transpose(0, 3, 1, 2, 4)

    q, k, v, log_a = to_chunks(q), to_chunks(k), to_chunks(v), to_chunks(log_a)
    q = q * (GLA_DK ** -0.5)
    b = jnp.cumsum(log_a, axis=3)
    b_ref = b[:, :, :, C // 2:C // 2 + 1, :]
    b_last = b[:, :, :, C - 1:C, :]
    q_in = q * jnp.exp(b - b_ref)
    k_in = k * jnp.exp(b_ref - b)
    scores = jnp.einsum('bhnid,bhnjd->bhnij', q_in, k_in)
    causal = jnp.tril(jnp.ones((C, C), dtype=bool))
    scores = jnp.where(causal, scores, 0.0)
    o_intra = jnp.einsum('bhnij,bhnjv->bhniv', scores, v)
    kv = jnp.einsum('bhnjd,bhnjv->bhndv', k * jnp.exp(b_last - b), v)
    decay = jnp.exp(b_last[:, :, :, 0, :])

    def step(S, inp):
        d_n, kv_n = inp
        return d_n[..., None] * S + kv_n, S

    S0 = jnp.zeros((B, GLA_HEADS, GLA_DK, GLA_DV), jnp.float32)
    _, S_prev = lax.scan(step, S0, (decay.transpose(2, 0, 1, 3), kv.transpose(2, 0, 1, 3, 4)))
    o_inter = jnp.einsum('bhnid,nbhdv->bhniv', q * jnp.exp(b), S_prev)
    o = o_intra + o_inter
    return o.transpose(0, 2, 3, 1, 4).reshape(B, L, GLA_HEADS, GLA_DV)


def hybrid_mixer(x, w_in, w_a2, b_a, gla_norm_g, conv_w, conv_b, conv_ln_g, conv_ln_b, w_out):
    B, L, _ = x.shape
    f32 = jnp.float32
    proj = x @ w_in
    q, k, v, r, a_lr, c = jnp.split(proj, [SPLIT_Q, SPLIT_K, SPLIT_V, SPLIT_R, SPLIT_A], axis=-1)
    log_a = jax.nn.log_sigmoid((a_lr @ w_a2 + b_a).astype(f32)) / GLA_TAU
    o = gla_chunked(q.astype(f32).reshape(B, L, GLA_HEADS, GLA_DK),
                    k.astype(f32).reshape(B, L, GLA_HEADS, GLA_DK),
                    v.astype(f32).reshape(B, L, GLA_HEADS, GLA_DV),
                    log_a.reshape(B, L, GLA_HEADS, GLA_DK))
    o = layer_norm(o, gla_norm_g.reshape(GLA_HEADS, GLA_DV))
    gla_out = (o.reshape(B, L, GLA_VAL_WIDTH) * jax.nn.silu(r.astype(f32))).astype(x.dtype)
    c_val, c_gate = jnp.split(c, 2, axis=-1)
    c = c_val * jax.nn.sigmoid(c_gate)
    c = lax.conv_general_dilated(
        c, conv_w[:, None, :], window_strides=(1,), padding=[(CONV_KERNEL - 1, 0)],
        dimension_numbers=('NWC', 'WIO', 'NWC'), feature_group_count=CONV_WIDTH) + conv_b
    conv_out = jax.nn.silu(layer_norm(c, conv_ln_g, conv_ln_b)).astype(x.dtype)
    return jnp.concatenate([gla_out, conv_out], axis=-1) @ w_out


def peer(h, w_q, sub_k1, sub_k2, u_tab, v_tab):
    B, L, D = h.shape
    T = B * L
    half = PEER_QDIM // 2
    hf = h.reshape(T, D)
    q = (hf @ w_q).reshape(T, PEER_HEADS, PEER_QDIM)
    s1 = jnp.einsum('thd,hkd->thk', q[..., :half], sub_k1).astype(jnp.float32)
    s2 = jnp.einsum('thd,hkd->thk', q[..., half:], sub_k2).astype(jnp.float32)
    s1t, i1 = lax.top_k(s1, PEER_TOPK)
    s2t, i2 = lax.top_k(s2, PEER_TOPK)
    cand = (s1t[..., :, None] + s2t[..., None, :]).reshape(T, PEER_HEADS, PEER_TOPK * PEER_TOPK)
    cidx = (i1[..., :, None] * PEER_NKEYS + i2[..., None, :]).reshape(T, PEER_HEADS, PEER_TOPK * PEER_TOPK)
    sc, pos = lax.top_k(cand, PEER_TOPK)
    eidx = jnp.take_along_axis(cidx, pos, axis=-1)
    gates = jax.nn.softmax(sc, axis=-1)
    n_blk = T // PEER_BLOCK
    HK = PEER_HEADS * PEER_TOPK

    def block(args):
        xb, ib, gb = args
        u = jnp.take(u_tab, ib, axis=0)
        act = jax.nn.gelu(jnp.einsum('pd,pkd->pk', xb, u).astype(jnp.float32), approximate=False)
        w = (gb * act).astype(xb.dtype)
        vv = jnp.take(v_tab, ib, axis=0)
        return jnp.einsum('pk,pkd->pd', w, vv)

    out = lax.map(block, (hf.reshape(n_blk, PEER_BLOCK, D),
                          eidx.reshape(n_blk, PEER_BLOCK, HK),
                          gates.reshape(n_blk, PEER_BLOCK, HK)))
    return out.reshape(B, L, D)


def setup_inputs(seed: int = 0) -> dict:
    key = jax.random.key(seed)
    ks = jax.random.split(key, 20)

    def nrm(k, shape, scale):
        return jax.random.normal(k, shape, jnp.float32) * scale

    return {
        "x": nrm(ks[0], (BATCH, SEQ, D_MODEL), 1.0),
        "w_in": nrm(ks[1], (DEPTH, D_MODEL, IN_WIDTH), D_MODEL ** -0.5),
        "w_a2": nrm(ks[2], (DEPTH, GLA_GATE_RANK, GLA_KEY_WIDTH), GLA_GATE_RANK ** -0.5),
        "b_a": nrm(ks[3], (DEPTH, GLA_KEY_WIDTH), 0.1),
        "gla_norm_g": 1.0 + nrm(ks[4], (DEPTH, GLA_VAL_WIDTH), 0.02),
        "conv_w": nrm(ks[5], (DEPTH, CONV_KERNEL, CONV_WIDTH), CONV_KERNEL ** -0.5),
        "conv_b": nrm(ks[6], (DEPTH, CONV_WIDTH), 0.02),
        "conv_ln_g": 1.0 + nrm(ks[7], (DEPTH, CONV_WIDTH), 0.02),
        "conv_ln_b": nrm(ks[8], (DEPTH, CONV_WIDTH), 0.02),
        "w_out": nrm(ks[9], (DEPTH, MIX_WIDTH, D_MODEL), DEEPNORM_BETA * MIX_WIDTH ** -0.5),
        "ln1_g": 1.0 + nrm(ks[10], (DEPTH, D_MODEL), 0.02),
        "ln1_b": nrm(ks[11], (DEPTH, D_MODEL), 0.02),
        "w_q": nrm(ks[12], (DEPTH, D_MODEL, PEER_HEADS * PEER_QDIM), D_MODEL ** -0.5),
        "sub_k1": nrm(ks[13], (DEPTH, PEER_HEADS, PEER_NKEYS, PEER_QDIM // 2), (PEER_QDIM // 2) ** -0.5),
        "sub_k2": nrm(ks[14], (DEPTH, PEER_HEADS, PEER_NKEYS, PEER_QDIM // 2), (PEER_QDIM // 2) ** -0.5),
        "u_tab": nrm(ks[15], (DEPTH, PEER_EXPERTS, D_MODEL), D_MODEL ** -0.5),
        "v_tab": nrm(ks[16], (DEPTH, PEER_EXPERTS, D_MODEL), DEEPNORM_BETA),
        "ln2_g": 1.0 + nrm(ks[17], (DEPTH, D_MODEL), 0.02),
        "ln2_b": nrm(ks[18], (DEPTH, D_MODEL), 0.02),
    }


def reference(x, w_in, w_a2, b_a, gla_norm_g, conv_w, conv_b, conv_ln_g, conv_ln_b, w_out,
              ln1_g, ln1_b, w_q, sub_k1, sub_k2, u_tab, v_tab, ln2_g, ln2_b):
    for l in range(DEPTH):
        mix = hybrid_mixer(x, w_in[l], w_a2[l], b_a[l], gla_norm_g[l], conv_w[l], conv_b[l],
                           conv_ln_g[l], conv_ln_b[l], w_out[l])
        h = layer_norm(DEEPNORM_ALPHA * x + mix, ln1_g[l], ln1_b[l]).astype(x.dtype)
        ffn = peer(h, w_q[l], sub_k1[l], sub_k2[l], u_tab[l], v_tab[l])
        x = layer_norm(DEEPNORM_ALPHA * h + ffn, ln2_g[l], ln2_b[l]).astype(x.dtype)
    return x
```

```python
import functools
import math

import jax
import jax.numpy as jnp
from jax import lax
from jax.experimental import pallas as pl
from jax.experimental.pallas import tpu as pltpu

f32 = jnp.float32
bf16 = jnp.bfloat16
i32 = jnp.int32

LANES = 128
SUBLANES = 8
VMEM_LIMIT_BYTES = 56 * 1024 * 1024

GLA_HEADS = 4
GLA_DK = 128
GLA_DV = 256
GLA_GATE_RANK = 16
GLA_TAU = 16.0
GLA_CHUNK = 64
CONV_KERNEL = 31
CONV_HALO = 32
PEER_HEADS = 8
PEER_NKEYS = 128
PEER_TOPK = 16
PEER_PAIRS = PEER_HEADS * PEER_TOPK
PEER_CHUNK = SUBLANES
PEER_NCHUNK = PEER_PAIRS // PEER_CHUNK
LN_EPS = 1e-5


def _cparams(*sem):
    return pltpu.CompilerParams(dimension_semantics=sem, vmem_limit_bytes=VMEM_LIMIT_BYTES)


def _layer_norm_rows(x, g, b=None):
    mu = jnp.mean(x, axis=-1, keepdims=True)
    xc = x - mu
    var = jnp.mean(xc * xc, axis=-1, keepdims=True)
    out = xc * lax.rsqrt(var + LN_EPS) * g
    if b is not None:
        out = out + b
    return out


def _silu(x):
    return x * jax.nn.sigmoid(x)


def _inproj_kernel(x_ref, w_ref, wa_ref, o_ref, a_ref, xb_ref):
    @pl.when(pl.program_id(1) == 0)
    def _():
        xb = x_ref[...].astype(bf16)
        xb_ref[...] = xb
        a_ref[...] = jnp.dot(xb, wa_ref[...], preferred_element_type=f32)

    o_ref[...] = jnp.dot(xb_ref[...], w_ref[...], preferred_element_type=f32)


def _inproj(x2, w_main, w_a, tm=512, tn=1024):
    T, D = x2.shape
    N = w_main.shape[1]
    return pl.pallas_call(
        _inproj_kernel,
        grid=(T // tm, N // tn),
        in_specs=[pl.BlockSpec((tm, D), lambda i, j: (i, 0)),
                  pl.BlockSpec((D, tn), lambda i, j: (0, j)),
                  pl.BlockSpec((D, LANES), lambda i, j: (0, 0))],
        out_specs=[pl.BlockSpec((tm, tn), lambda i, j: (i, j)),
                   pl.BlockSpec((tm, LANES), lambda i, j: (i, 0))],
        out_shape=[jax.ShapeDtypeStruct((T, N), f32), jax.ShapeDtypeStruct((T, LANES), f32)],
        scratch_shapes=[pltpu.VMEM((tm, D), bf16)],
        compiler_params=_cparams("parallel", "arbitrary"),
    )(x2, w_main, w_a)


def _gla_kernel(q_ref, k_ref, v_ref, r_ref, a_ref, wa2_ref, ba_ref, g_ref, o_ref, st_ref):
    C = GLA_CHUNK
    L = q_ref.shape[1]
    st_ref[...] = jnp.zeros_like(st_ref)
    row = lax.broadcasted_iota(i32, (C, C), 0)
    col = lax.broadcasted_iota(i32, (C, C), 1)
    causal = row >= col
    tril = causal.astype(f32)
    scale = GLA_DK ** -0.5
    wa2 = wa2_ref[...]
    ba = ba_ref[...]
    g = g_ref[...]

    def chunk(n, carry):
        rows = pl.ds(pl.multiple_of(n * C, C), C)
        z = jnp.dot(a_ref[0, rows, :], wa2, preferred_element_type=f32,
                    precision=lax.Precision.HIGHEST) + ba
        log_a = (jnp.minimum(z, 0.0) - jnp.log1p(jnp.exp(-jnp.abs(z)))) / GLA_TAU
        b = jnp.dot(tril, log_a, preferred_element_type=f32, precision=lax.Precision.HIGHEST)
        b_ref = b[C // 2:C // 2 + 1, :]
        b_last = b[C - 1:C, :]
        q = q_ref[0, rows, :] * scale
        k = k_ref[0, rows, :]
        v = v_ref[0, rows, :].astype(bf16)
        q_in = (q * jnp.exp(b - b_ref)).astype(bf16)
        k_in = (k * jnp.exp(b_ref - b)).astype(bf16)
        scores = lax.dot_general(q_in, k_in, (((1,), (1,)), ((), ())), preferred_element_type=f32)
        scores = jnp.where(causal, scores, 0.0)
        o = jnp.dot(scores.astype(bf16), v, preferred_element_type=f32)
        st = st_ref[...]
        q_dec = (q * jnp.exp(b)).astype(bf16)
        o = o + lax.dot_general(q_dec, st.astype(bf16), (((1,), (1,)), ((), ())),
                                preferred_element_type=f32)
        k_dec = (k * jnp.exp(b_last - b)).astype(bf16)
        kv_t = lax.dot_general(v, k_dec, (((0,), (0,)), ((), ())), preferred_element_type=f32)
        st_ref[...] = st * jnp.exp(b_last) + kv_t
        o = _layer_norm_rows(o, g)
        o_ref[0, rows, :] = (o * _silu(r_ref[0, rows, :])).astype(o_ref.dtype)
        return carry

    lax.fori_loop(0, L // C, chunk, 0)


def _gla(proj3, a3, w_a2p, b_a, gla_g):
    B, L, _ = proj3.shape
    H, DK, DV = GLA_HEADS, GLA_DK, GLA_DV
    kq = (H * DK) // DK
    kv = (2 * H * DK) // DV
    kr = kv + H
    return pl.pallas_call(
        _gla_kernel,
        grid=(B, H),
        in_specs=[pl.BlockSpec((1, L, DK), lambda b, h: (b, 0, h)),
                  pl.BlockSpec((1, L, DK), lambda b, h: (b, 0, kq + h)),
                  pl.BlockSpec((1, L, DV), lambda b, h: (b, 0, kv + h)),
                  pl.BlockSpec((1, L, DV), lambda b, h: (b, 0, kr + h)),
                  pl.BlockSpec((1, L, LANES), lambda b, h: (b, 0, 0)),
                  pl.BlockSpec((LANES, DK), lambda b, h: (0, h)),
                  pl.BlockSpec((1, DK), lambda b, h: (0, h)),
                  pl.BlockSpec((1, DV), lambda b, h: (0, h))],
        out_specs=pl.BlockSpec((1, L, DV), lambda b, h: (b, 0, h)),
        out_shape=jax.ShapeDtypeStruct((B, L, H * DV), bf16),
        scratch_shapes=[pltpu.VMEM((DV, DK), f32)],
        compiler_params=_cparams("parallel", "parallel"),
    )(proj3, proj3, proj3, proj3, a3, w_a2p, b_a, gla_g)


def _conv_kernel(val_ref, gate_ref, w_ref, cb_ref, g_ref, b_ref, o_ref, glu_ref, acc_ref):
    tl = val_ref.shape[1]
    W = val_ref.shape[2]
    rc = 64

    @pl.when(pl.program_id(1) == 0)
    def _():
        glu_ref[0:CONV_HALO, :] = jnp.zeros((CONV_HALO, W), f32)

    @pl.when(pl.program_id(1) != 0)
    def _():
        glu_ref[0:CONV_HALO, :] = glu_ref[tl:tl + CONV_HALO, :]

    glu_ref[CONV_HALO:CONV_HALO + tl, :] = val_ref[0] * jax.nn.sigmoid(gate_ref[0])
    lead = CONV_HALO - (CONV_KERNEL - 1)

    def strip(s, carry):
        cols = pl.ds(pl.multiple_of(s * LANES, LANES), LANES)
        wj = w_ref[:, cols]
        bias = cb_ref[:, cols]
        for r0 in range(0, tl, rc):
            acc = jnp.zeros((rc, LANES), f32) + bias
            for j in range(CONV_KERNEL):
                acc = acc + glu_ref[pl.ds(r0 + lead + j, rc), cols] * wj[j:j + 1, :]
            acc_ref[pl.ds(r0, rc), cols] = acc
        return carry

    lax.fori_loop(0, W // LANES, strip, 0)
    y = _layer_norm_rows(acc_ref[...], g_ref[...], b_ref[...])
    o_ref[0] = _silu(y).astype(o_ref.dtype)


def _conv(proj3, conv_wp, conv_b, ln_g, ln_b, val_blk, tl=256):
    B, L, _ = proj3.shape
    W = conv_wp.shape[1]
    return pl.pallas_call(
        _conv_kernel,
        grid=(B, L // tl),
        in_specs=[pl.BlockSpec((1, tl, W), lambda b, l: (b, l, val_blk)),
                  pl.BlockSpec((1, tl, W), lambda b, l: (b, l, val_blk + 1)),
                  pl.BlockSpec(conv_wp.shape, lambda b, l: (0, 0)),
                  pl.BlockSpec((1, W), lambda b, l: (0, 0)),
                  pl.BlockSpec((1, W), lambda b, l: (0, 0)),
                  pl.BlockSpec((1, W), lambda b, l: (0, 0))],
        out_specs=pl.BlockSpec((1, tl, W), lambda b, l: (b, l, 0)),
        out_shape=jax.ShapeDtypeStruct((B, L, W), bf16),
        scratch_shapes=[pltpu.VMEM((tl + CONV_HALO, W), f32), pltpu.VMEM((tl, W), f32)],
        compiler_params=_cparams("parallel", "arbitrary"),
    )(proj3, proj3, conv_wp, conv_b, ln_g, ln_b)


def _outproj_kernel(alpha, ga_ref, cv_ref, x_ref, w1_ref, w2_ref, g_ref, b_ref, h_ref):
    mix = jnp.dot(ga_ref[...], w1_ref[...], preferred_element_type=f32)
    mix = mix + jnp.dot(cv_ref[...], w2_ref[...], preferred_element_type=f32)
    h_ref[...] = _layer_norm_rows(alpha * x_ref[...] + mix, g_ref[...], b_ref[...])


def _outproj(gla_out, conv_out, x2, w_o1, w_o2, g, b, alpha, tm=256):
    T, D = x2.shape
    W1 = gla_out.shape[1]
    W2 = conv_out.shape[1]
    return pl.pallas_call(
        functools.partial(_outproj_kernel, alpha),
        grid=(T // tm,),
        in_specs=[pl.BlockSpec((tm, W1), lambda i: (i, 0)),
                  pl.BlockSpec((tm, W2), lambda i: (i, 0)),
                  pl.BlockSpec((tm, D), lambda i: (i, 0)),
                  pl.BlockSpec((W1, D), lambda i: (0, 0)),
                  pl.BlockSpec((W2, D), lambda i: (0, 0)),
                  pl.BlockSpec((1, D), lambda i: (0, 0)),
                  pl.BlockSpec((1, D), lambda i: (0, 0))],
        out_specs=pl.BlockSpec((tm, D), lambda i: (i, 0)),
        out_shape=jax.ShapeDtypeStruct((T, D), f32),
        compiler_params=_cparams("parallel"),
    )(gla_out, conv_out, x2, w_o1, w_o2, g, b)


def _qproj_kernel(h_ref, w_ref, q_ref):
    q_ref[...] = jnp.dot(h_ref[...].astype(bf16), w_ref[...],
                         preferred_element_type=f32).astype(q_ref.dtype)


def _qproj(h2, w_q, tm=256):
    T, D = h2.shape
    N = w_q.shape[1]
    return pl.pallas_call(
        _qproj_kernel,
        grid=(T // tm,),
        in_specs=[pl.BlockSpec((tm, D), lambda i: (i, 0)),
                  pl.BlockSpec((D, N), lambda i: (0, 0))],
        out_specs=pl.BlockSpec((tm, N), lambda i: (i, 0)),
        out_shape=jax.ShapeDtypeStruct((T, N), bf16),
        compiler_params=_cparams("parallel"),
    )(h2, w_q)


def _topk_rows(s, k):
    R = s.shape[0]
    iota = lax.broadcasted_iota(i32, s.shape, 0)
    vals, idxs = [], []
    for _ in range(k):
        m = jnp.max(s, axis=0, keepdims=True)
        idx = jnp.min(jnp.where(s == m, iota, R), axis=0, keepdims=True)
        vals.append(m)
        idxs.append(idx)
        s = jnp.where(iota == idx, -jnp.inf, s)
    return vals, idxs


def _bitonic_sort_groups(keys, vals):
    ng = len(keys)
    n = ng * SUBLANES
    sub = lax.broadcasted_iota(i32, keys[0].shape, 0)
    k = 2
    while k <= n:
        j = k // 2
        while j >= 1:
            if j >= SUBLANES:
                gj = j // SUBLANES
                for g in range(ng):
                    if g & gj:
                        continue
                    p = g | gj
                    asc = ((g * SUBLANES) & k) == 0
                    swap = keys[g] > keys[p] if asc else keys[g] < keys[p]
                    kg = jnp.where(swap, keys[p], keys[g])
                    kp = jnp.where(swap, keys[g], keys[p])
                    vg = jnp.where(swap, vals[p], vals[g])
                    vp = jnp.where(swap, vals[g], vals[p])
                    keys[g], keys[p], vals[g], vals[p] = kg, kp, vg, vp
            else:
                low = (sub & j) == 0
                for g in range(ng):
                    x, y = keys[g], vals[g]
                    px = jnp.where(low, pltpu.roll(x, SUBLANES - j, 0), pltpu.roll(x, j, 0))
                    py = jnp.where(low, pltpu.roll(y, SUBLANES - j, 0), pltpu.roll(y, j, 0))
                    if k >= SUBLANES:
                        asc = ((g * SUBLANES) & k) == 0
                        take_min = low if asc else jnp.logical_not(low)
                    else:
                        take_min = low == ((sub & k) == 0)
                    sel = (take_min & (px < x)) | (jnp.logical_not(take_min) & (px > x))
                    keys[g] = jnp.where(sel, px, x)
                    vals[g] = jnp.where(sel, py, y)
            j //= 2
        k *= 2
    return keys, vals


def _route_kernel(nb, q_ref, k1_ref, k2_ref, e_ref, eloc_ref, gate_ref, st_ref, scr_i, scr_f):
    Tt = q_ref.shape[0]
    half = q_ref.shape[1] // PEER_HEADS // 2
    bsz = (PEER_NKEYS * PEER_NKEYS) // nb
    nt = (((1,), (1,)), ((), ()))
    e_rows, g_rows = [], []
    for h in range(PEER_HEADS):
        q1 = q_ref[:, (2 * h) * half:(2 * h + 1) * half]
        q2 = q_ref[:, (2 * h + 1) * half:(2 * h + 2) * half]
        s1 = lax.dot_general(k1_ref[h], q1, nt, preferred_element_type=f32)
        s2 = lax.dot_general(k2_ref[h], q2, nt, preferred_element_type=f32)
        s1t, i1 = _topk_rows(s1, PEER_TOPK)
        s2t, i2 = _topk_rows(s2, PEER_TOPK)
        s2c = jnp.concatenate(s2t, axis=0)
        i2c = jnp.concatenate(i2, axis=0)
        cand = jnp.concatenate([s1t[a] + s2c for a in range(PEER_TOPK)], axis=0)
        cidx = jnp.concatenate([i1[a] * PEER_NKEYS + i2c for a in range(PEER_TOPK)], axis=0)
        iota = lax.broadcasted_iota(i32, cand.shape, 0)
        sc, ex = [], []
        for _ in range(PEER_TOPK):
            m = jnp.max(cand, axis=0, keepdims=True)
            pos = jnp.min(jnp.where(cand == m, iota, cand.shape[0]), axis=0, keepdims=True)
            hit = iota == pos
            ex.append(jnp.sum(jnp.where(hit, cidx, 0), axis=0, keepdims=True))
            sc.append(m)
            cand = jnp.where(hit, -jnp.inf, cand)
        p = [jnp.exp(s - sc[0]) for s in sc]
        denom = p[0]
        for t in p[1:]:
            denom = denom + t
        e_rows += ex
        g_rows += [t / denom for t in p]
    ng = PEER_PAIRS // SUBLANES
    keys = [jnp.concatenate(e_rows[g * SUBLANES:(g + 1) * SUBLANES], axis=0) for g in range(ng)]
    vals = [jnp.concatenate(g_rows[g * SUBLANES:(g + 1) * SUBLANES], axis=0) for g in range(ng)]
    keys, vals = _bitonic_sort_groups(keys, vals)
    for c in range(ng):
        scr_i[pl.ds(c, SUBLANES, stride=PEER_NCHUNK), :] = keys[c]
        scr_f[pl.ds(c, SUBLANES, stride=PEER_NCHUNK), :] = vals[c]
    e_all = scr_i[...]
    e_tok = e_all.T
    e_ref[...] = e_tok
    gate_ref[...] = scr_f[...].T
    for b in range(nb):
        eloc_ref[b] = jnp.clip(e_tok - b * bsz, 0, bsz - 1)
    lane = lax.broadcasted_iota(i32, (Tt, LANES), 1)
    starts = jnp.zeros((Tt, LANES), i32)
    for b in range(1, nb + 1):
        cnt = jnp.sum((e_tok < b * bsz).astype(i32), axis=1, keepdims=True)
        starts = jnp.where(lane == b, cnt, starts)
    st_ref[...] = starts


def _route(q, k1, k2, nb, tt=128):
    T, QW = q.shape
    H, NK, half = k1.shape
    return pl.pallas_call(
        functools.partial(_route_kernel, nb),
        grid=(T // tt,),
        in_specs=[pl.BlockSpec((tt, QW), lambda i: (i, 0)),
                  pl.BlockSpec((H, NK, half), lambda i: (0, 0, 0)),
                  pl.BlockSpec((H, NK, half), lambda i: (0, 0, 0))],
        out_specs=[pl.BlockSpec((tt, PEER_PAIRS), lambda i: (i, 0)),
                   pl.BlockSpec((nb, tt, PEER_PAIRS), lambda i: (0, i, 0)),
                   pl.BlockSpec((tt, PEER_PAIRS), lambda i: (i, 0)),
                   pl.BlockSpec((tt, LANES), lambda i: (i, 0))],
        out_shape=[jax.ShapeDtypeStruct((T, PEER_PAIRS), i32),
                   jax.ShapeDtypeStruct((nb, T, PEER_PAIRS), i32),
                   jax.ShapeDtypeStruct((T, PEER_PAIRS), f32),
                   jax.ShapeDtypeStruct((T, LANES), i32)],
        scratch_shapes=[pltpu.VMEM((PEER_PAIRS, tt), i32), pltpu.VMEM((PEER_PAIRS, tt), f32)],
        compiler_params=_cparams("parallel"),
    )(q, k1, k2)


def _chunk_range(st_smem, t, b):
    s0 = st_smem[t, b]
    s1 = st_smem[t, b + 1]
    cs = lax.shift_right_logical(s0, 3)
    ce = jnp.where(s1 > s0, lax.shift_right_logical(s1 + (PEER_CHUNK - 1), 3), cs)
    return cs, ce


def _gelu_exact(x):
    return 0.5 * x * (1.0 + lax.erf(x * (2.0 ** -0.5)))


def _act_kernel(nb, eloc_hbm, st_hbm, h_ref, u_ref, e_ref, gate_ref, w_ref,
                e_smem, st_smem, act_ref, sem):
    i = pl.program_id(0)
    b = pl.program_id(1)
    Tk = h_ref.shape[0]
    bsz = u_ref.shape[0]
    shift = int(math.log2(bsz))
    cp_e = pltpu.make_async_copy(eloc_hbm.at[b, pl.ds(i * Tk, Tk)], e_smem, sem.at[0])
    cp_s = pltpu.make_async_copy(st_hbm.at[pl.ds(i * Tk, Tk)], st_smem, sem.at[1])
    cp_e.start()
    cp_s.start()

    @pl.when(b == 0)
    def _():
        act_ref[...] = jnp.zeros_like(act_ref)

    cp_e.wait()
    cp_s.wait()
    lane = lax.broadcasted_iota(i32, (SUBLANES, LANES), 1)
    sub = lax.broadcasted_iota(i32, (SUBLANES, LANES), 0)
    lane_chunk = lane % PEER_NCHUNK
    diag = sub == lane // PEER_NCHUNK

    def token(t, carry):
        cs, ce = _chunk_range(st_smem, t, b)
        hrow = h_ref[t]

        def chunk(c, R):
            rows = []
            for k in range(PEER_CHUNK):
                e = e_smem[t, k * PEER_NCHUNK + c]
                pr = hrow * u_ref[e].astype(f32)
                rows.append(jnp.sum(pr, axis=0, keepdims=True))
            q = jnp.concatenate(rows, axis=0)
            r = jnp.sum(q, axis=1, keepdims=True)
            return jnp.where(lane_chunk == c, r, R)

        R = lax.fori_loop(cs, ce, chunk, jnp.zeros((SUBLANES, LANES), f32))
        row = jnp.sum(jnp.where(diag, R, 0.0), axis=0, keepdims=True)
        mine = lax.shift_right_logical(e_ref[pl.ds(t, 1), :], shift) == b
        act_ref[pl.ds(t, 1), :] = jnp.where(mine, row, act_ref[pl.ds(t, 1), :])
        return carry

    lax.fori_loop(0, Tk, token, 0)

    @pl.when(b == nb - 1)
    def _():
        w = gate_ref[...] * _gelu_exact(act_ref[...])
        blk = lax.shift_right_logical(e_ref[...], shift)
        for bb in range(nb):
            w_ref[bb] = jnp.where(blk == bb, w, 0.0)


def _peer_act(eloc, starts, h3, u3, e_tok, gate, nb, tk):
    T = h3.shape[0]
    E = u3.shape[0]
    bsz = E // nb
    rowblk = h3.shape[1:]
    return pl.pallas_call(
        functools.partial(_act_kernel, nb),
        grid=(T // tk, nb),
        in_specs=[pl.BlockSpec(memory_space=pl.ANY),
                  pl.BlockSpec(memory_space=pl.ANY),
                  pl.BlockSpec((tk,) + rowblk, lambda i, b: (i, 0, 0)),
                  pl.BlockSpec((bsz,) + rowblk, lambda i, b: (b, 0, 0)),
                  pl.BlockSpec((tk, PEER_PAIRS), lambda i, b: (i, 0)),
                  pl.BlockSpec((tk, PEER_PAIRS), lambda i, b: (i, 0))],
        out_specs=pl.BlockSpec((nb, tk, PEER_PAIRS), lambda i, b: (0, i, 0)),
        out_shape=jax.ShapeDtypeStruct((nb, T, PEER_PAIRS), f32),
        scratch_shapes=[pltpu.SMEM((tk, PEER_PAIRS), i32),
                        pltpu.SMEM((tk, LANES), i32),
                        pltpu.VMEM((tk, PEER_PAIRS), f32),
                        pltpu.SemaphoreType.DMA((2,))],
        compiler_params=_cparams("arbitrary", "arbitrary"),
    )(eloc, starts, h3, u3, e_tok, gate)


def _val_kernel(eloc_hbm, st_hbm, w_hbm, v_ref, o_ref, e_smem, st_smem, w_smem, sem):
    i = pl.program_id(0)
    b = pl.program_id(1)
    Tk = o_ref.shape[0]
    cp_e = pltpu.make_async_copy(eloc_hbm.at[b, pl.ds(i * Tk, Tk)], e_smem, sem.at[0])
    cp_s = pltpu.make_async_copy(st_hbm.at[pl.ds(i * Tk, Tk)], st_smem, sem.at[1])
    cp_w = pltpu.make_async_copy(w_hbm.at[b, pl.ds(i * Tk, Tk)], w_smem, sem.at[2])
    cp_e.start()
    cp_s.start()
    cp_w.start()

    @pl.when(b == 0)
    def _():
        o_ref[...] = jnp.zeros_like(o_ref)

    cp_e.wait()
    cp_s.wait()
    cp_w.wait()

    def token(t, carry):
        cs, ce = _chunk_range(st_smem, t, b)

        def chunk(c, acc):
            for k in range(PEER_CHUNK):
                l = k * PEER_NCHUNK + c
                acc = acc + w_smem[t, l] * v_ref[e_smem[t, l]].astype(f32)
            return acc

        acc = lax.fori_loop(cs, ce, chunk, jnp.zeros(o_ref.shape[1:], f32))
        o_ref[t] = o_ref[t] + acc
        return carry

    lax.fori_loop(0, Tk, token, 0)


def _peer_val(eloc, starts, w, v3, nb, tk):
    T = starts.shape[0]
    E = v3.shape[0]
    bsz = E // nb
    rowblk = v3.shape[1:]
    return pl.pallas_call(
        _val_kernel,
        grid=(T // tk, nb),
        in_specs=[pl.BlockSpec(memory_space=pl.ANY),
                  pl.BlockSpec(memory_space=pl.ANY),
                  pl.BlockSpec(memory_space=pl.ANY),
                  pl.BlockSpec((bsz,) + rowblk, lambda i, b: (b, 0, 0))],
        out_specs=pl.BlockSpec((tk,) + rowblk, lambda i, b: (i, 0, 0)),
        out_shape=jax.ShapeDtypeStruct((T,) + rowblk, f32),
        scratch_shapes=[pltpu.SMEM((tk, PEER_PAIRS), i32),
                        pltpu.SMEM((tk, LANES), i32),
                        pltpu.SMEM((tk, PEER_PAIRS), f32),
                        pltpu.SemaphoreType.DMA((3,))],
        compiler_params=_cparams("arbitrary", "arbitrary"),
    )(eloc, starts, w, v3)


def _ln2_kernel(alpha, h_ref, f_ref, g_ref, b_ref, o_ref):
    o_ref[...] = _layer_norm_rows(alpha * h_ref[...] + f_ref[...], g_ref[...], b_ref[...])


def _ln2(h2, ffn, g, b, alpha, tm=512):
    T, D = h2.shape
    return pl.pallas_call(
        functools.partial(_ln2_kernel, alpha),
        grid=(T // tm,),
        in_specs=[pl.BlockSpec((tm, D), lambda i: (i, 0)),
                  pl.BlockSpec((tm, D), lambda i: (i, 0)),
                  pl.BlockSpec((1, D), lambda i: (0, 0)),
                  pl.BlockSpec((1, D), lambda i: (0, 0))],
        out_specs=pl.BlockSpec((tm, D), lambda i: (i, 0)),
        out_shape=jax.ShapeDtypeStruct((T, D), f32),
        compiler_params=_cparams("parallel"),
    )(h2, ffn, g, b)


PEER_NBLOCKS = 4
PEER_TOKENS = 256


def _layer(x, w_in, w_a2, b_a, gla_norm_g, conv_w, conv_b, conv_ln_g, conv_ln_b, w_out,
           ln1_g, ln1_b, w_q, sub_k1, sub_k2, u_tab, v_tab, ln2_g, ln2_b, alpha):
    B, L, D = x.shape
    T = B * L
    kw = GLA_HEADS * GLA_DK
    vw = GLA_HEADS * GLA_DV
    split_r = 2 * kw + 2 * vw
    split_a = split_r + GLA_GATE_RANK
    cw = conv_w.shape[1]
    x2 = x.reshape(T, D)
    w_main = jnp.concatenate([w_in[:, :split_r], w_in[:, split_a:]], axis=1).astype(bf16)
    w_a = jnp.pad(w_in[:, split_r:split_a], ((0, 0), (0, LANES - GLA_GATE_RANK))).astype(bf16)
    w_a2p = jnp.pad(w_a2, ((0, LANES - GLA_GATE_RANK), (0, 0)))
    conv_wp = jnp.pad(conv_w, ((0, CONV_HALO - CONV_KERNEL), (0, 0)))
    proj, a_lr = _inproj(x2, w_main, w_a)
    proj3 = proj.reshape(B, L, -1)
    gla_out = _gla(proj3, a_lr.reshape(B, L, LANES), w_a2p, b_a.reshape(1, kw),
                   gla_norm_g.reshape(1, vw))
    conv_out = _conv(proj3, conv_wp, conv_b.reshape(1, cw), conv_ln_g.reshape(1, cw),
                     conv_ln_b.reshape(1, cw), val_blk=split_r // cw)
    h2 = _outproj(gla_out.reshape(T, vw), conv_out.reshape(T, cw), x2,
                  w_out[:vw].astype(bf16), w_out[vw:].astype(bf16),
                  ln1_g.reshape(1, D), ln1_b.reshape(1, D), alpha)
    q = _qproj(h2, w_q.astype(bf16))
    nb = PEER_NBLOCKS
    e_tok, eloc, gate, starts = _route(q, sub_k1.astype(bf16), sub_k2.astype(bf16), nb)
    rows = D // LANES
    u3 = u_tab.astype(bf16).reshape(-1, rows, LANES)
    v3 = v_tab.astype(bf16).reshape(-1, rows, LANES)
    h3 = h2.reshape(T, rows, LANES)
    w = _peer_act(eloc, starts, h3, u3, e_tok, gate, nb, PEER_TOKENS)
    ffn = _peer_val(eloc, starts, w, v3, nb, PEER_TOKENS)
    return _ln2(h2, ffn.reshape(T, D), ln2_g.reshape(1, D), ln2_b.reshape(1, D), alpha).reshape(B, L, D)


def kernel(x, w_in, w_a2, b_a, gla_norm_g, conv_w, conv_b, conv_ln_g, conv_ln_b, w_out, ln1_g, ln1_b, w_q, sub_k1, sub_k2, u_tab, v_tab, ln2_g, ln2_b):
    depth = w_in.shape[0]
    alpha = (2.0 * depth) ** 0.25
    for l in range(depth):
        x = _layer(x, w_in[l], w_a2[l], b_a[l], gla_norm_g[l], conv_w[l], conv_b[l], conv_ln_g[l],
                   conv_ln_b[l], w_out[l], ln1_g[l], ln1_b[l], w_q[l], sub_k1[l], sub_k2[l],
                   u_tab[l], v_tab[l], ln2_g[l], ln2_b[l], alpha)
    return x
```

```python
import functools
import math

import jax
import jax.numpy as jnp
from jax import lax
from jax.experimental import pallas as pl
from jax.experimental.pallas import tpu as pltpu

f32 = jnp.float32
bf16 = jnp.bfloat16
i32 = jnp.int32

LANES = 128
SUBLANES = 8
VMEM_LIMIT_BYTES = 56 * 1024 * 1024

GLA_HEADS = 4
GLA_DK = 128
GLA_DV = 256
GLA_GATE_RANK = 16
GLA_TAU = 16.0
GLA_CHUNK = 64
CONV_KERNEL = 31
CONV_HALO = 32
PEER_HEADS = 8
PEER_NKEYS = 128
PEER_TOPK = 16
PEER_PAIRS = PEER_HEADS * PEER_TOPK
PEER_CHUNK = SUBLANES
PEER_NCHUNK = PEER_PAIRS // PEER_CHUNK
LN_EPS = 1e-5


def _cparams(*sem):
    return pltpu.CompilerParams(dimension_semantics=sem, vmem_limit_bytes=VMEM_LIMIT_BYTES)


def _layer_norm_rows(x, g, b=None):
    mu = jnp.mean(x, axis=-1, keepdims=True)
    xc = x - mu
    var = jnp.mean(xc * xc, axis=-1, keepdims=True)
    out = xc * lax.rsqrt(var + LN_EPS) * g
    if b is not None:
        out = out + b
    return out


def _silu(x):
    return x * jax.nn.sigmoid(x)


def _inproj_kernel(x_ref, w_ref, wa_ref, o_ref, a_ref, xb_ref):
    @pl.when(pl.program_id(1) == 0)
    def _():
        xb = x_ref[...].astype(bf16)
        xb_ref[...] = xb
        a_ref[...] = jnp.dot(xb, wa_ref[...], preferred_element_type=f32)

    o_ref[...] = jnp.dot(xb_ref[...], w_ref[...], preferred_element_type=f32)


def _inproj(x2, w_main, w_a, tm=512, tn=1024):
    T, D = x2.shape
    N = w_main.shape[1]
    return pl.pallas_call(
        _inproj_kernel,
        grid=(T // tm, N // tn),
        in_specs=[pl.BlockSpec((tm, D), lambda i, j: (i, 0)),
                  pl.BlockSpec((D, tn), lambda i, j: (0, j)),
                  pl.BlockSpec((D, LANES), lambda i, j: (0, 0))],
        out_specs=[pl.BlockSpec((tm, tn), lambda i, j: (i, j)),
                   pl.BlockSpec((tm, LANES), lambda i, j: (i, 0))],
        out_shape=[jax.ShapeDtypeStruct((T, N), f32), jax.ShapeDtypeStruct((T, LANES), f32)],
        scratch_shapes=[pltpu.VMEM((tm, D), bf16)],
        compiler_params=_cparams("parallel", "arbitrary"),
    )(x2, w_main, w_a)


def _gla_kernel(q_ref, k_ref, v_ref, r_ref, a_ref, wa2_ref, ba_ref, g_ref, o_ref, st_ref):
    C = GLA_CHUNK
    L = q_ref.shape[1]
    st_ref[...] = jnp.zeros_like(st_ref)
    row = lax.broadcasted_iota(i32, (C, C), 0)
    col = lax.broadcasted_iota(i32, (C, C), 1)
    causal = row >= col
    tril = causal.astype(f32)
    scale = GLA_DK ** -0.5
    wa2 = wa2_ref[...]
    ba = ba_ref[...]
    g = g_ref[...]

    def chunk(n, carry):
        rows = pl.ds(pl.multiple_of(n * C, C), C)
        z = jnp.dot(a_ref[0, rows, :], wa2, preferred_element_type=f32,
                    precision=lax.Precision.HIGHEST) + ba
        log_a = (jnp.minimum(z, 0.0) - jnp.log1p(jnp.exp(-jnp.abs(z)))) / GLA_TAU
        b = jnp.dot(tril, log_a, preferred_element_type=f32, precision=lax.Precision.HIGHEST)
        b_ref = b[C // 2:C // 2 + 1, :]
        b_last = b[C - 1:C, :]
        q = q_ref[0, rows, :] * scale
        k = k_ref[0, rows, :]
        v = v_ref[0, rows, :].astype(bf16)
        q_in = (q * jnp.exp(b - b_ref)).astype(bf16)
        k_in = (k * jnp.exp(b_ref - b)).astype(bf16)
        scores = lax.dot_general(q_in, k_in, (((1,), (1,)), ((), ())), preferred_element_type=f32)
        scores = jnp.where(causal, scores, 0.0)
        o = jnp.dot(scores.astype(bf16), v, preferred_element_type=f32)
        st = st_ref[...]
        q_dec = (q * jnp.exp(b)).astype(bf16)
        o = o + lax.dot_general(q_dec, st.astype(bf16), (((1,), (1,)), ((), ())),
                                preferred_element_type=f32)
        k_dec = (k * jnp.exp(b_last - b)).astype(bf16)
        kv_t = lax.dot_general(v, k_dec, (((0,), (0,)), ((), ())), preferred_element_type=f32)
        st_ref[...] = st * jnp.exp(b_last) + kv_t
        o = _layer_norm_rows(o, g)
        o_ref[0, rows, :] = (o * _silu(r_ref[0, rows, :])).astype(o_ref.dtype)
        return carry

    lax.fori_loop(0, L // C, chunk, 0)


def _gla(proj3, a3, w_a2p, b_a, gla_g):
    B, L, _ = proj3.shape
    H, DK, DV = GLA_HEADS, GLA_DK, GLA_DV
    kq = (H * DK) // DK
    kv = (2 * H * DK) // DV
    kr = kv + H
    return pl.pallas_call(
        _gla_kernel,
        grid=(B, H),
        in_specs=[pl.BlockSpec((1, L, DK), lambda b, h: (b, 0, h)),
                  pl.BlockSpec((1, L, DK), lambda b, h: (b, 0, kq + h)),
                  pl.BlockSpec((1, L, DV), lambda b, h: (b, 0, kv + h)),
                  pl.BlockSpec((1, L, DV), lambda b, h: (b, 0, kr + h)),
                  pl.BlockSpec((1, L, LANES), lambda b, h: (b, 0, 0)),
                  pl.BlockSpec((LANES, DK), lambda b, h: (0, h)),
                  pl.BlockSpec((1, DK), lambda b, h: (0, h)),
                  pl.BlockSpec((1, DV), lambda b, h: (0, h))],
        out_specs=pl.BlockSpec((1, L, DV), lambda b, h: (b, 0, h)),
        out_shape=jax.ShapeDtypeStruct((B, L, H * DV), bf16),
        scratch_shapes=[pltpu.VMEM((DV, DK), f32)],
        compiler_params=_cparams("parallel", "parallel"),
    )(proj3, proj3, proj3, proj3, a3, w_a2p, b_a, gla_g)


def _conv_kernel(val_ref, gate_ref, w_ref, cb_ref, g_ref, b_ref, o_ref, glu_ref, acc_ref):
    tl = val_ref.shape[1]
    W = val_ref.shape[2]
    rc = 64

    @pl.when(pl.program_id(1) == 0)
    def _():
        glu_ref[0:CONV_HALO, :] = jnp.zeros((CONV_HALO, W), f32)

    @pl.when(pl.program_id(1) != 0)
    def _():
        glu_ref[0:CONV_HALO, :] = glu_ref[tl:tl + CONV_HALO, :]

    glu_ref[CONV_HALO:CONV_HALO + tl, :] = val_ref[0] * jax.nn.sigmoid(gate_ref[0])
    lead = CONV_HALO - (CONV_KERNEL - 1)

    def strip(s, carry):
        cols = pl.ds(pl.multiple_of(s * LANES, LANES), LANES)
        wj = w_ref[:, cols]
        bias = cb_ref[:, cols]
        for r0 in range(0, tl, rc):
            acc = jnp.zeros((rc, LANES), f32) + bias
            for j in range(CONV_KERNEL):
                acc = acc + glu_ref[pl.ds(r0 + lead + j, rc), cols] * wj[j:j + 1, :]
            acc_ref[pl.ds(r0, rc), cols] = acc
        return carry

    lax.fori_loop(0, W // LANES, strip, 0)
    y = _layer_norm_rows(acc_ref[...], g_ref[...], b_ref[...])
    o_ref[0] = _silu(y).astype(o_ref.dtype)


def _conv(proj3, conv_wp, conv_b, ln_g, ln_b, val_blk, tl=256):
    B, L, _ = proj3.shape
    W = conv_wp.shape[1]
    return pl.pallas_call(
        _conv_kernel,
        grid=(B, L // tl),
        in_specs=[pl.BlockSpec((1, tl, W), lambda b, l: (b, l, val_blk)),
                  pl.BlockSpec((1, tl, W), lambda b, l: (b, l, val_blk + 1)),
                  pl.BlockSpec(conv_wp.shape, lambda b, l: (0, 0)),
                  pl.BlockSpec((1, W), lambda b, l: (0, 0)),
                  pl.BlockSpec((1, W), lambda b, l: (0, 0)),
                  pl.BlockSpec((1, W), lambda b, l: (0, 0))],
        out_specs=pl.BlockSpec((1, tl, W), lambda b, l: (b, l, 0)),
        out_shape=jax.ShapeDtypeStruct((B, L, W), bf16),
        scratch_shapes=[pltpu.VMEM((tl + CONV_HALO, W), f32), pltpu.VMEM((tl, W), f32)],
        compiler_params=_cparams("parallel", "arbitrary"),
    )(proj3, proj3, conv_wp, conv_b, ln_g, ln_b)


def _outproj_kernel(alpha, ga_ref, cv_ref, x_ref, w1_ref, w2_ref, g_ref, b_ref, h_ref):
    mix = jnp.dot(ga_ref[...], w1_ref[...], preferred_element_type=f32)
    mix = mix + jnp.dot(cv_ref[...], w2_ref[...], preferred_element_type=f32)
    h_ref[...] = _layer_norm_rows(alpha * x_ref[...] + mix, g_ref[...], b_ref[...])


def _outproj(gla_out, conv_out, x2, w_o1, w_o2, g, b, alpha, tm=256):
    T, D = x2.shape
    W1 = gla_out.shape[1]
    W2 = conv_out.shape[1]
    return pl.pallas_call(
        functools.partial(_outproj_kernel, alpha),
        grid=(T // tm,),
        in_specs=[pl.BlockSpec((tm, W1), lambda i: (i, 0)),
                  pl.BlockSpec((tm, W2), lambda i: (i, 0)),
                  pl.BlockSpec((tm, D), lambda i: (i, 0)),
                  pl.BlockSpec((W1, D), lambda i: (0, 0)),
                  pl.BlockSpec((W2, D), lambda i: (0, 0)),
                  pl.BlockSpec((1, D), lambda i: (0, 0)),
                  pl.BlockSpec((1, D), lambda i: (0, 0))],
        out_specs=pl.BlockSpec((tm, D), lambda i: (i, 0)),
        out_shape=jax.ShapeDtypeStruct((T, D), f32),
        compiler_params=_cparams("parallel"),
    )(gla_out, conv_out, x2, w_o1, w_o2, g, b)


def _qproj_kernel(h_ref, w_ref, q_ref):
    q_ref[...] = jnp.dot(h_ref[...].astype(bf16), w_ref[...],
                         preferred_element_type=f32).astype(q_ref.dtype)


def _qproj(h2, w_q, tm=256):
    T, D = h2.shape
    N = w_q.shape[1]
    return pl.pallas_call(
        _qproj_kernel,
        grid=(T // tm,),
        in_specs=[pl.BlockSpec((tm, D), lambda i: (i, 0)),
                  pl.BlockSpec((D, N), lambda i: (0, 0))],
        out_specs=pl.BlockSpec((tm, N), lambda i: (i, 0)),
        out_shape=jax.ShapeDtypeStruct((T, N), bf16),
        compiler_params=_cparams("parallel"),
    )(h2, w_q)


def _topk_rows(s, k):
    R = s.shape[0]
    iota = lax.broadcasted_iota(i32, s.shape, 0)
    vals, idxs = [], []
    for _ in range(k):
        m = jnp.max(s, axis=0, keepdims=True)
        idx = jnp.min(jnp.where(s == m, iota, R), axis=0, keepdims=True)
        vals.append(m)
        idxs.append(idx)
        s = jnp.where(iota == idx, -jnp.inf, s)
    return vals, idxs


def _bitonic_sort_groups(keys, vals):
    ng = len(keys)
    n = ng * SUBLANES
    sub = lax.broadcasted_iota(i32, keys[0].shape, 0)
    k = 2
    while k <= n:
        j = k // 2
        while j >= 1:
            if j >= SUBLANES:
                gj = j // SUBLANES
                for g in range(ng):
                    if g & gj:
                        continue
                    p = g | gj
                    asc = ((g * SUBLANES) & k) == 0
                    swap = keys[g] > keys[p] if asc else keys[g] < keys[p]
                    kg = jnp.where(swap, keys[p], keys[g])
                    kp = jnp.where(swap, keys[g], keys[p])
                    vg = jnp.where(swap, vals[p], vals[g])
                    vp = jnp.where(swap, vals[g], vals[p])
                    keys[g], keys[p], vals[g], vals[p] = kg, kp, vg, vp
            else:
                low = (sub & j) == 0
                for g in range(ng):
                    x, y = keys[g], vals[g]
                    px = jnp.where(low, pltpu.roll(x, SUBLANES - j, 0), pltpu.roll(x, j, 0))
                    py = jnp.where(low, pltpu.roll(y, SUBLANES - j, 0), pltpu.roll(y, j, 0))
                    if k >= SUBLANES:
                        asc = ((g * SUBLANES) & k) == 0
                        take_min = low if asc else jnp.logical_not(low)
                    else:
                        take_min = low == ((sub & k) == 0)
                    sel = (take_min & (px < x)) | (jnp.logical_not(take_min) & (px > x))
                    keys[g] = jnp.where(sel, px, x)
                    vals[g] = jnp.where(sel, py, y)
            j //= 2
        k *= 2
    return keys, vals


def _route_kernel(nb, q_ref, k1_ref, k2_ref, e_ref, eloc_ref, gate_ref, st_ref):
    Tt = q_ref.shape[0]
    half = q_ref.shape[1] // PEER_HEADS // 2
    bsz = (PEER_NKEYS * PEER_NKEYS) // nb
    nt = (((1,), (1,)), ((), ()))
    e_rows, g_rows = [], []
    for h in range(PEER_HEADS):
        q1 = q_ref[:, (2 * h) * half:(2 * h + 1) * half]
        q2 = q_ref[:, (2 * h + 1) * half:(2 * h + 2) * half]
        s1 = lax.dot_general(k1_ref[h], q1, nt, preferred_element_type=f32)
        s2 = lax.dot_general(k2_ref[h], q2, nt, preferred_element_type=f32)
        s1t, i1 = _topk_rows(s1, PEER_TOPK)
        s2t, i2 = _topk_rows(s2, PEER_TOPK)
        s2c = jnp.concatenate(s2t, axis=0)
        i2c = jnp.concatenate(i2, axis=0)
        cand = jnp.concatenate([s1t[a] + s2c for a in range(PEER_TOPK)], axis=0)
        cidx = jnp.concatenate([i1[a] * PEER_NKEYS + i2c for a in range(PEER_TOPK)], axis=0)
        iota = lax.broadcasted_iota(i32, cand.shape, 0)
        sc, ex = [], []
        for _ in range(PEER_TOPK):
            m = jnp.max(cand, axis=0, keepdims=True)
            pos = jnp.min(jnp.where(cand == m, iota, cand.shape[0]), axis=0, keepdims=True)
            hit = iota == pos
            ex.append(jnp.sum(jnp.where(hit, cidx, 0), axis=0, keepdims=True))
            sc.append(m)
            cand = jnp.where(hit, -jnp.inf, cand)
        p = [jnp.exp(s - sc[0]) for s in sc]
        denom = p[0]
        for t in p[1:]:
            denom = denom + t
        e_rows += ex
        g_rows += [t / denom for t in p]
    ng = PEER_PAIRS // SUBLANES
    keys = [jnp.concatenate(e_rows[g * SUBLANES:(g + 1) * SUBLANES], axis=0) for g in range(ng)]
    vals = [jnp.concatenate(g_rows[g * SUBLANES:(g + 1) * SUBLANES], axis=0) for g in range(ng)]
    keys, vals = _bitonic_sort_groups(keys, vals)
    e_tok = jnp.concatenate(keys, axis=0).T
    e_ref[...] = e_tok
    gate_ref[...] = jnp.concatenate(vals, axis=0).T
    for b in range(nb):
        eloc_ref[b] = jnp.clip(e_tok - b * bsz, 0, bsz - 1)
    lane = lax.broadcasted_iota(i32, (Tt, LANES), 1)
    starts = jnp.zeros((Tt, LANES), i32)
    for b in range(1, nb + 1):
        cnt = jnp.sum((e_tok < b * bsz).astype(i32), axis=1, keepdims=True)
        starts = jnp.where(lane == b, cnt, starts)
    st_ref[...] = starts


def _route(q, k1, k2, nb, tt=128):
    T, QW = q.shape
    H, NK, half = k1.shape
    return pl.pallas_call(
        functools.partial(_route_kernel, nb),
        grid=(T // tt,),
        in_specs=[pl.BlockSpec((tt, QW), lambda i: (i, 0)),
                  pl.BlockSpec((H, NK, half), lambda i: (0, 0, 0)),
                  pl.BlockSpec((H, NK, half), lambda i: (0, 0, 0))],
        out_specs=[pl.BlockSpec((tt, PEER_PAIRS), lambda i: (i, 0)),
                   pl.BlockSpec((nb, tt, PEER_PAIRS), lambda i: (0, i, 0)),
                   pl.BlockSpec((tt, PEER_PAIRS), lambda i: (i, 0)),
                   pl.BlockSpec((tt, LANES), lambda i: (i, 0))],
        out_shape=[jax.ShapeDtypeStruct((T, PEER_PAIRS), i32),
                   jax.ShapeDtypeStruct((nb, T, PEER_PAIRS), i32),
                   jax.ShapeDtypeStruct((T, PEER_PAIRS), f32),
                   jax.ShapeDtypeStruct((T, LANES), i32)],
        compiler_params=_cparams("parallel"),
    )(q, k1, k2)


PEER_ST_STRIDE = SUBLANES


def _chunk_range(st_smem, t, b):
    s0 = st_smem[t * PEER_ST_STRIDE + b]
    s1 = st_smem[t * PEER_ST_STRIDE + b + 1]
    cs = lax.shift_right_logical(s0, 3)
    ce = jnp.where(s1 > s0, lax.shift_right_logical(s1 + (PEER_CHUNK - 1), 3), cs)
    return s0, s1, cs, ce


def _gelu_exact(x):
    return 0.5 * x * (1.0 + lax.erf(x * (2.0 ** -0.5)))


def _act_kernel(nb, eloc_hbm, st_hbm, h_ref, u_ref, e_ref, gate_ref, w_ref,
                e_smem, st_smem, stage_ref, act_ref, sem):
    i = pl.program_id(0)
    b = pl.program_id(1)
    Tk = h_ref.shape[0]
    T = Tk * pl.num_programs(0)
    shift = int(math.log2(u_ref.shape[0]))
    cp_e = pltpu.make_async_copy(
        eloc_hbm.at[pl.ds((b * T + i * Tk) * PEER_PAIRS, Tk * PEER_PAIRS)], e_smem, sem.at[0])
    cp_s = pltpu.make_async_copy(
        st_hbm.at[pl.ds(i * Tk * PEER_ST_STRIDE, Tk * PEER_ST_STRIDE)], st_smem, sem.at[1])
    cp_e.start()
    cp_s.start()
    cp_e.wait()
    cp_s.wait()
    sub = lax.broadcasted_iota(i32, (SUBLANES, LANES), 0)

    def token(t, carry):
        s0, s1, cs, ce = _chunk_range(st_smem, t, b)
        hrow = h_ref[t]
        base = t * PEER_PAIRS

        def chunk(c, carry2):
            first = pl.multiple_of(c * PEER_CHUNK, PEER_CHUNK)
            rows = []
            for k in range(PEER_CHUNK):
                pr = hrow * u_ref[e_smem[base + first + k]].astype(f32)
                rows.append(jnp.sum(pr, axis=0, keepdims=True))
            part = jnp.concatenate(rows, axis=0)
            pos = sub + first
            pltpu.store(stage_ref.at[t, pl.ds(first, PEER_CHUNK), :], part,
                        mask=(pos >= s0) & (pos < s1))
            return carry2

        lax.fori_loop(cs, ce, chunk, 0)
        return carry

    lax.fori_loop(0, Tk, token, 0)

    @pl.when(b == nb - 1)
    def _():
        def reduce_token(t, carry):
            act_ref[pl.ds(t, 1), :] = jnp.sum(stage_ref[t].T, axis=0, keepdims=True)
            return carry

        lax.fori_loop(0, Tk, reduce_token, 0)
        w = gate_ref[...] * _gelu_exact(act_ref[...])
        blk = lax.shift_right_logical(e_ref[...], shift)
        for bb in range(nb):
            w_ref[bb] = jnp.where(blk == bb, w, 0.0)


def _peer_act(eloc, starts, h3, u3, e_tok, gate, nb, tk):
    T = h3.shape[0]
    E = u3.shape[0]
    bsz = E // nb
    rowblk = h3.shape[1:]
    return pl.pallas_call(
        functools.partial(_act_kernel, nb),
        grid=(T // tk, nb),
        in_specs=[pl.BlockSpec(memory_space=pl.ANY),
                  pl.BlockSpec(memory_space=pl.ANY),
                  pl.BlockSpec((tk,) + rowblk, lambda i, b: (i, 0, 0)),
                  pl.BlockSpec((bsz,) + rowblk, lambda i, b: (b, 0, 0)),
                  pl.BlockSpec((tk, PEER_PAIRS), lambda i, b: (i, 0)),
                  pl.BlockSpec((tk, PEER_PAIRS), lambda i, b: (i, 0))],
        out_specs=pl.BlockSpec((nb, tk, PEER_PAIRS), lambda i, b: (0, i, 0)),
        out_shape=jax.ShapeDtypeStruct((nb, T, PEER_PAIRS), f32),
        scratch_shapes=[pltpu.SMEM((tk * PEER_PAIRS,), i32),
                        pltpu.SMEM((tk * PEER_ST_STRIDE,), i32),
                        pltpu.VMEM((tk, PEER_PAIRS, LANES), f32),
                        pltpu.VMEM((tk, PEER_PAIRS), f32),
                        pltpu.SemaphoreType.DMA((2,))],
        compiler_params=_cparams("arbitrary", "arbitrary"),
    )(eloc, starts, h3, u3, e_tok, gate)


def _val_kernel(eloc_hbm, st_hbm, w_hbm, v_ref, o_ref, e_smem, st_smem, w_smem, sem):
    i = pl.program_id(0)
    b = pl.program_id(1)
    Tk = o_ref.shape[0]
    T = Tk * pl.num_programs(0)
    lists = pl.ds((b * T + i * Tk) * PEER_PAIRS, Tk * PEER_PAIRS)
    cp_e = pltpu.make_async_copy(eloc_hbm.at[lists], e_smem, sem.at[0])
    cp_s = pltpu.make_async_copy(
        st_hbm.at[pl.ds(i * Tk * PEER_ST_STRIDE, Tk * PEER_ST_STRIDE)], st_smem, sem.at[1])
    cp_w = pltpu.make_async_copy(w_hbm.at[lists], w_smem, sem.at[2])
    cp_e.start()
    cp_s.start()
    cp_w.start()

    @pl.when(b == 0)
    def _():
        o_ref[...] = jnp.zeros_like(o_ref)

    cp_e.wait()
    cp_s.wait()
    cp_w.wait()

    def token(t, carry):
        _, _, cs, ce = _chunk_range(st_smem, t, b)
        base = t * PEER_PAIRS

        def chunk(c, acc):
            first = base + c * PEER_CHUNK
            for k in range(PEER_CHUNK):
                acc = acc + w_smem[first + k] * v_ref[e_smem[first + k]].astype(f32)
            return acc

        acc = lax.fori_loop(cs, ce, chunk, jnp.zeros(o_ref.shape[1:], f32))
        o_ref[t] = o_ref[t] + acc
        return carry

    lax.fori_loop(0, Tk, token, 0)


def _peer_val(eloc, starts, w, v3, nb, tk):
    T = eloc.shape[0] // (nb * PEER_PAIRS)
    E = v3.shape[0]
    bsz = E // nb
    rowblk = v3.shape[1:]
    return pl.pallas_call(
        _val_kernel,
        grid=(T // tk, nb),
        in_specs=[pl.BlockSpec(memory_space=pl.ANY),
                  pl.BlockSpec(memory_space=pl.ANY),
                  pl.BlockSpec(memory_space=pl.ANY),
                  pl.BlockSpec((bsz,) + rowblk, lambda i, b: (b, 0, 0))],
        out_specs=pl.BlockSpec((tk,) + rowblk, lambda i, b: (i, 0, 0)),
        out_shape=jax.ShapeDtypeStruct((T,) + rowblk, f32),
        scratch_shapes=[pltpu.SMEM((tk * PEER_PAIRS,), i32),
                        pltpu.SMEM((tk * PEER_ST_STRIDE,), i32),
                        pltpu.SMEM((tk * PEER_PAIRS,), f32),
                        pltpu.SemaphoreType.DMA((3,))],
        compiler_params=_cparams("arbitrary", "arbitrary"),
    )(eloc, starts, w, v3)


def _ln2_kernel(alpha, h_ref, f_ref, g_ref, b_ref, o_ref):
    o_ref[...] = _layer_norm_rows(alpha * h_ref[...] + f_ref[...], g_ref[...], b_ref[...])


def _ln2(h2, ffn, g, b, alpha, tm=512):
    T, D = h2.shape
    return pl.pallas_call(
        functools.partial(_ln2_kernel, alpha),
        grid=(T // tm,),
        in_specs=[pl.BlockSpec((tm, D), lambda i: (i, 0)),
                  pl.BlockSpec((tm, D), lambda i: (i, 0)),
                  pl.BlockSpec((1, D), lambda i: (0, 0)),
                  pl.BlockSpec((1, D), lambda i: (0, 0))],
        out_specs=pl.BlockSpec((tm, D), lambda i: (i, 0)),
        out_shape=jax.ShapeDtypeStruct((T, D), f32),
        compiler_params=_cparams("parallel"),
    )(h2, ffn, g, b)


PEER_NBLOCKS = 4
PEER_ACT_TOKENS = 128
PEER_VAL_TOKENS = 256


def _layer(x, w_in, w_a2, b_a, gla_norm_g, conv_w, conv_b, conv_ln_g, conv_ln_b, w_out,
           ln1_g, ln1_b, w_q, sub_k1, sub_k2, u_tab, v_tab, ln2_g, ln2_b, alpha):
    B, L, D = x.shape
    T = B * L
    kw = GLA_HEADS * GLA_DK
    vw = GLA_HEADS * GLA_DV
    split_r = 2 * kw + 2 * vw
    split_a = split_r + GLA_GATE_RANK
    cw = conv_w.shape[1]
    x2 = x.reshape(T, D)
    w_main = jnp.concatenate([w_in[:, :split_r], w_in[:, split_a:]], axis=1).astype(bf16)
    w_a = jnp.pad(w_in[:, split_r:split_a], ((0, 0), (0, LANES - GLA_GATE_RANK))).astype(bf16)
    w_a2p = jnp.pad(w_a2, ((0, LANES - GLA_GATE_RANK), (0, 0)))
    conv_wp = jnp.pad(conv_w, ((0, CONV_HALO - CONV_KERNEL), (0, 0)))
    proj, a_lr = _inproj(x2, w_main, w_a)
    proj3 = proj.reshape(B, L, -1)
    gla_out = _gla(proj3, a_lr.reshape(B, L, LANES), w_a2p, b_a.reshape(1, kw),
                   gla_norm_g.reshape(1, vw))
    conv_out = _conv(proj3, conv_wp, conv_b.reshape(1, cw), conv_ln_g.reshape(1, cw),
                     conv_ln_b.reshape(1, cw), val_blk=split_r // cw)
    h2 = _outproj(gla_out.reshape(T, vw), conv_out.reshape(T, cw), x2,
                  w_out[:vw].astype(bf16), w_out[vw:].astype(bf16),
                  ln1_g.reshape(1, D), ln1_b.reshape(1, D), alpha)
    q = _qproj(h2, w_q.astype(bf16))
    nb = PEER_NBLOCKS
    e_tok, eloc, gate, starts = _route(q, sub_k1.astype(bf16), sub_k2.astype(bf16), nb)
    rows = D // LANES
    u3 = u_tab.astype(bf16).reshape(-1, rows, LANES)
    v3 = v_tab.astype(bf16).reshape(-1, rows, LANES)
    h3 = h2.reshape(T, rows, LANES)
    eloc = eloc.reshape(-1)
    starts = starts[:, :PEER_ST_STRIDE].reshape(-1)
    w = _peer_act(eloc, starts, h3, u3, e_tok, gate, nb, PEER_ACT_TOKENS)
    ffn = _peer_val(eloc, starts, w.reshape(-1), v3, nb, PEER_VAL_TOKENS)
    return _ln2(h2, ffn.reshape(T, D), ln2_g.reshape(1, D), ln2_b.reshape(1, D), alpha).reshape(B, L, D)


def kernel(x, w_in, w_a2, b_a, gla_norm_g, conv_w, conv_b, conv_ln_g, conv_ln_b, w_out, ln1_g, ln1_b, w_q, sub_k1, sub_k2, u_tab, v_tab, ln2_g, ln2_b):
    depth = w_in.shape[0]
    alpha = (2.0 * depth) ** 0.25
    for l in range(depth):
        x = _layer(x, w_in[l], w_a2[l], b_a[l], gla_norm_g[l], conv_w[l], conv_b[l], conv_ln_g[l],
                   conv_ln_b[l], w_out[l], ln1_g[l], ln1_b[l], w_q[l], sub_k1[l], sub_k2[l],
                   u_tab[l], v_tab[l], ln2_g[l], ln2_b[l], alpha)
    return x
```

```python
import functools
import math

import jax
import jax.numpy as jnp
from jax import lax
from jax.experimental import pallas as pl
from jax.experimental.pallas import tpu as pltpu

f32 = jnp.float32
bf16 = jnp.bfloat16
i32 = jnp.int32

LANES = 128
SUBLANES = 8
VMEM_LIMIT_BYTES = 56 * 1024 * 1024

GLA_HEADS = 4
GLA_DK = 128
GLA_DV = 256
GLA_GATE_RANK = 16
GLA_TAU = 16.0
GLA_CHUNK = 64
CONV_KERNEL = 31
CONV_HALO = 32
PEER_HEADS = 8
PEER_NKEYS = 128
PEER_TOPK = 16
PEER_PAIRS = PEER_HEADS * PEER_TOPK
PEER_CHUNK = SUBLANES
PEER_NCHUNK = PEER_PAIRS // PEER_CHUNK
LN_EPS = 1e-5


def _cparams(*sem):
    return pltpu.CompilerParams(dimension_semantics=sem, vmem_limit_bytes=VMEM_LIMIT_BYTES)


def _layer_norm_rows(x, g, b=None):
    mu = jnp.mean(x, axis=-1, keepdims=True)
    xc = x - mu
    var = jnp.mean(xc * xc, axis=-1, keepdims=True)
    out = xc * lax.rsqrt(var + LN_EPS) * g
    if b is not None:
        out = out + b
    return out


def _silu(x):
    return x * jax.nn.sigmoid(x)


def _inproj_kernel(x_ref, w_ref, wa_ref, o_ref, a_ref, xb_ref):
    @pl.when(pl.program_id(1) == 0)
    def _():
        xb = x_ref[...].astype(bf16)
        xb_ref[...] = xb
        a_ref[...] = jnp.dot(xb, wa_ref[...], preferred_element_type=f32)

    o_ref[...] = jnp.dot(xb_ref[...], w_ref[...], preferred_element_type=f32)


def _inproj(x2, w_main, w_a, tm=512, tn=1024):
    T, D = x2.shape
    N = w_main.shape[1]
    return pl.pallas_call(
        _inproj_kernel,
        grid=(T // tm, N // tn),
        in_specs=[pl.BlockSpec((tm, D), lambda i, j: (i, 0)),
                  pl.BlockSpec((D, tn), lambda i, j: (0, j)),
                  pl.BlockSpec((D, LANES), lambda i, j: (0, 0))],
        out_specs=[pl.BlockSpec((tm, tn), lambda i, j: (i, j)),
                   pl.BlockSpec((tm, LANES), lambda i, j: (i, 0))],
        out_shape=[jax.ShapeDtypeStruct((T, N), f32), jax.ShapeDtypeStruct((T, LANES), f32)],
        scratch_shapes=[pltpu.VMEM((tm, D), bf16)],
        compiler_params=_cparams("parallel", "arbitrary"),
    )(x2, w_main, w_a)


def _gla_kernel(q_ref, k_ref, v_ref, r_ref, a_ref, wa2_ref, ba_ref, g_ref, o_ref, st_ref):
    C = GLA_CHUNK
    L = q_ref.shape[1]
    st_ref[...] = jnp.zeros_like(st_ref)
    row = lax.broadcasted_iota(i32, (C, C), 0)
    col = lax.broadcasted_iota(i32, (C, C), 1)
    causal = row >= col
    tril = causal.astype(f32)
    scale = GLA_DK ** -0.5
    DK, DV = GLA_DK, GLA_DV
    heads = q_ref.shape[2] // DK
    nt = (((1,), (1,)), ((), ()))

    def chunk(n, carry):
        rows = pl.ds(pl.multiple_of(n * C, C), C)
        a_lr = a_ref[0, rows, :]
        for hh in range(heads):
            kc = slice(hh * DK, (hh + 1) * DK)
            vc = slice(hh * DV, (hh + 1) * DV)
            z = jnp.dot(a_lr, wa2_ref[:, kc], preferred_element_type=f32,
                        precision=lax.Precision.HIGHEST) + ba_ref[:, kc]
            log_a = (jnp.minimum(z, 0.0) - jnp.log1p(jnp.exp(-jnp.abs(z)))) / GLA_TAU
            b = jnp.dot(tril, log_a, preferred_element_type=f32, precision=lax.Precision.HIGHEST)
            b_ref = b[C // 2:C // 2 + 1, :]
            b_last = b[C - 1:C, :]
            q = q_ref[0, rows, kc] * scale
            k = k_ref[0, rows, kc]
            v = v_ref[0, rows, vc].astype(bf16)
            q_in = (q * jnp.exp(b - b_ref)).astype(bf16)
            k_in = (k * jnp.exp(b_ref - b)).astype(bf16)
            scores = lax.dot_general(q_in, k_in, nt, preferred_element_type=f32)
            scores = jnp.where(causal, scores, 0.0)
            o = jnp.dot(scores.astype(bf16), v, preferred_element_type=f32)
            st = st_ref[hh]
            q_dec = (q * jnp.exp(b)).astype(bf16)
            o = o + lax.dot_general(q_dec, st.astype(bf16), nt, preferred_element_type=f32)
            k_dec = (k * jnp.exp(b_last - b)).astype(bf16)
            kv_t = lax.dot_general(v, k_dec, (((0,), (0,)), ((), ())), preferred_element_type=f32)
            st_ref[hh] = st * jnp.exp(b_last) + kv_t
            o = _layer_norm_rows(o, g_ref[:, vc])
            o_ref[0, rows, vc] = (o * _silu(r_ref[0, rows, vc])).astype(o_ref.dtype)
        return carry

    lax.fori_loop(0, L // C, chunk, 0)


def _gla(proj3, a3, w_a2p, b_a, gla_g, heads_per_step=2):
    B, L, _ = proj3.shape
    H, hs = GLA_HEADS, heads_per_step
    DK, DV = hs * GLA_DK, hs * GLA_DV
    kq = H // hs
    kv = 2 * kq * DK // DV
    kr = kv + H // hs
    return pl.pallas_call(
        _gla_kernel,
        grid=(B, H // hs),
        in_specs=[pl.BlockSpec((1, L, DK), lambda b, h: (b, 0, h)),
                  pl.BlockSpec((1, L, DK), lambda b, h: (b, 0, kq + h)),
                  pl.BlockSpec((1, L, DV), lambda b, h: (b, 0, kv + h)),
                  pl.BlockSpec((1, L, DV), lambda b, h: (b, 0, kr + h)),
                  pl.BlockSpec((1, L, LANES), lambda b, h: (b, 0, 0)),
                  pl.BlockSpec((LANES, DK), lambda b, h: (0, h)),
                  pl.BlockSpec((1, DK), lambda b, h: (0, h)),
                  pl.BlockSpec((1, DV), lambda b, h: (0, h))],
        out_specs=pl.BlockSpec((1, L, DV), lambda b, h: (b, 0, h)),
        out_shape=jax.ShapeDtypeStruct((B, L, H * GLA_DV), bf16),
        scratch_shapes=[pltpu.VMEM((hs, GLA_DV, GLA_DK), f32)],
        compiler_params=_cparams("parallel", "parallel"),
    )(proj3, proj3, proj3, proj3, a3, w_a2p, b_a, gla_g)


def _conv_kernel(val_ref, gate_ref, w_ref, cb_ref, g_ref, b_ref, o_ref, glu_ref, acc_ref):
    tl = val_ref.shape[1]
    W = val_ref.shape[2]
    rc = 64

    @pl.when(pl.program_id(1) == 0)
    def _():
        glu_ref[0:CONV_HALO, :] = jnp.zeros((CONV_HALO, W), f32)

    @pl.when(pl.program_id(1) != 0)
    def _():
        glu_ref[0:CONV_HALO, :] = glu_ref[tl:tl + CONV_HALO, :]

    glu_ref[CONV_HALO:CONV_HALO + tl, :] = val_ref[0] * jax.nn.sigmoid(gate_ref[0])
    lead = CONV_HALO - (CONV_KERNEL - 1)

    def strip(s, carry):
        cols = pl.ds(pl.multiple_of(s * LANES, LANES), LANES)
        wj = w_ref[:, cols]
        bias = cb_ref[:, cols]
        for r0 in range(0, tl, rc):
            acc = jnp.zeros((rc, LANES), f32) + bias
            for j in range(CONV_KERNEL):
                acc = acc + glu_ref[pl.ds(r0 + lead + j, rc), cols] * wj[j:j + 1, :]
            acc_ref[pl.ds(r0, rc), cols] = acc
        return carry

    lax.fori_loop(0, W // LANES, strip, 0)
    y = _layer_norm_rows(acc_ref[...], g_ref[...], b_ref[...])
    o_ref[0] = _silu(y).astype(o_ref.dtype)


def _conv(proj3, conv_wp, conv_b, ln_g, ln_b, val_blk, tl=256):
    B, L, _ = proj3.shape
    W = conv_wp.shape[1]
    return pl.pallas_call(
        _conv_kernel,
        grid=(B, L // tl),
        in_specs=[pl.BlockSpec((1, tl, W), lambda b, l: (b, l, val_blk)),
                  pl.BlockSpec((1, tl, W), lambda b, l: (b, l, val_blk + 1)),
                  pl.BlockSpec(conv_wp.shape, lambda b, l: (0, 0)),
                  pl.BlockSpec((1, W), lambda b, l: (0, 0)),
                  pl.BlockSpec((1, W), lambda b, l: (0, 0)),
                  pl.BlockSpec((1, W), lambda b, l: (0, 0))],
        out_specs=pl.BlockSpec((1, tl, W), lambda b, l: (b, l, 0)),
        out_shape=jax.ShapeDtypeStruct((B, L, W), bf16),
        scratch_shapes=[pltpu.VMEM((tl + CONV_HALO, W), f32), pltpu.VMEM((tl, W), f32)],
        compiler_params=_cparams("parallel", "arbitrary"),
    )(proj3, proj3, conv_wp, conv_b, ln_g, ln_b)


def _outproj_kernel(alpha, ga_ref, cv_ref, x_ref, w1_ref, w2_ref, g_ref, b_ref, h_ref):
    mix = jnp.dot(ga_ref[...], w1_ref[...], preferred_element_type=f32)
    mix = mix + jnp.dot(cv_ref[...], w2_ref[...], preferred_element_type=f32)
    h_ref[...] = _layer_norm_rows(alpha * x_ref[...] + mix, g_ref[...], b_ref[...])


def _outproj(gla_out, conv_out, x2, w_o1, w_o2, g, b, alpha, tm=256):
    T, D = x2.shape
    W1 = gla_out.shape[1]
    W2 = conv_out.shape[1]
    return pl.pallas_call(
        functools.partial(_outproj_kernel, alpha),
        grid=(T // tm,),
        in_specs=[pl.BlockSpec((tm, W1), lambda i: (i, 0)),
                  pl.BlockSpec((tm, W2), lambda i: (i, 0)),
                  pl.BlockSpec((tm, D), lambda i: (i, 0)),
                  pl.BlockSpec((W1, D), lambda i: (0, 0)),
                  pl.BlockSpec((W2, D), lambda i: (0, 0)),
                  pl.BlockSpec((1, D), lambda i: (0, 0)),
                  pl.BlockSpec((1, D), lambda i: (0, 0))],
        out_specs=pl.BlockSpec((tm, D), lambda i: (i, 0)),
        out_shape=jax.ShapeDtypeStruct((T, D), f32),
        compiler_params=_cparams("parallel"),
    )(gla_out, conv_out, x2, w_o1, w_o2, g, b)


def _qproj_kernel(h_ref, w_ref, q_ref):
    q_ref[...] = jnp.dot(h_ref[...].astype(bf16), w_ref[...],
                         preferred_element_type=f32).astype(q_ref.dtype)


def _qproj(h2, w_q, tm=256):
    T, D = h2.shape
    N = w_q.shape[1]
    return pl.pallas_call(
        _qproj_kernel,
        grid=(T // tm,),
        in_specs=[pl.BlockSpec((tm, D), lambda i: (i, 0)),
                  pl.BlockSpec((D, N), lambda i: (0, 0))],
        out_specs=pl.BlockSpec((tm, N), lambda i: (i, 0)),
        out_shape=jax.ShapeDtypeStruct((T, N), bf16),
        compiler_params=_cparams("parallel"),
    )(h2, w_q)


def _topk_rows(s, k):
    R = s.shape[0]
    iota = lax.broadcasted_iota(i32, s.shape, 0)
    vals, idxs = [], []
    for _ in range(k):
        m = jnp.max(s, axis=0, keepdims=True)
        idx = jnp.min(jnp.where(s == m, iota, R), axis=0, keepdims=True)
        vals.append(m)
        idxs.append(idx)
        s = jnp.where(iota == idx, -jnp.inf, s)
    return vals, idxs


def _staircase_candidates(s1t, i1, s2t, i2):
    K = PEER_TOPK
    s1c, i1c = jnp.concatenate(s1t, axis=0), jnp.concatenate(i1, axis=0)
    s2c, i2c = jnp.concatenate(s2t, axis=0), jnp.concatenate(i2, axis=0)
    sub = lax.broadcasted_iota(i32, (SUBLANES, s1c.shape[1]), 0)
    vals, idxs = [], []
    a = 0
    while K // (a + 1) > 1:
        n_b = K // (a + 1)
        for b0 in range(0, n_b, SUBLANES):
            v = s1t[a] + s2c[b0:b0 + SUBLANES]
            if n_b - b0 < SUBLANES:
                v = jnp.where(sub < n_b - b0, v, -jnp.inf)
            vals.append(v)
            idxs.append(i1[a] * PEER_NKEYS + i2c[b0:b0 + SUBLANES])
        a += 1
    assert K - a == SUBLANES
    vals.append(s1c[a:K] + s2t[0])
    idxs.append(i1c[a:K] * PEER_NKEYS + i2[0])
    return jnp.concatenate(vals, axis=0), jnp.concatenate(idxs, axis=0)


def _bitonic_sort_groups(keys, vals):
    ng = len(keys)
    n = ng * SUBLANES
    sub = lax.broadcasted_iota(i32, keys[0].shape, 0)
    k = 2
    while k <= n:
        j = k // 2
        while j >= 1:
            if j >= SUBLANES:
                gj = j // SUBLANES
                for g in range(ng):
                    if g & gj:
                        continue
                    p = g | gj
                    asc = ((g * SUBLANES) & k) == 0
                    swap = keys[g] > keys[p] if asc else keys[g] < keys[p]
                    kg = jnp.where(swap, keys[p], keys[g])
                    kp = jnp.where(swap, keys[g], keys[p])
                    vg = jnp.where(swap, vals[p], vals[g])
                    vp = jnp.where(swap, vals[g], vals[p])
                    keys[g], keys[p], vals[g], vals[p] = kg, kp, vg, vp
            else:
                low = (sub & j) == 0
                for g in range(ng):
                    x, y = keys[g], vals[g]
                    px = jnp.where(low, pltpu.roll(x, SUBLANES - j, 0), pltpu.roll(x, j, 0))
                    py = jnp.where(low, pltpu.roll(y, SUBLANES - j, 0), pltpu.roll(y, j, 0))
                    if k >= SUBLANES:
                        asc = ((g * SUBLANES) & k) == 0
                        take_min = low if asc else jnp.logical_not(low)
                    else:
                        take_min = low == ((sub & k) == 0)
                    sel = (take_min & (px < x)) | (jnp.logical_not(take_min) & (px > x))
                    keys[g] = jnp.where(sel, px, x)
                    vals[g] = jnp.where(sel, py, y)
            j //= 2
        k *= 2
    return keys, vals


def _route_kernel(nba, nbv, q_ref, k1_ref, k2_ref, e_ref, eloca_ref, elocv_ref, gate_ref, st_ref):
    Tt = q_ref.shape[0]
    half = q_ref.shape[1] // PEER_HEADS // 2
    nt = (((1,), (1,)), ((), ()))
    e_rows, g_rows = [], []
    for h in range(PEER_HEADS):
        q1 = q_ref[:, (2 * h) * half:(2 * h + 1) * half]
        q2 = q_ref[:, (2 * h + 1) * half:(2 * h + 2) * half]
        s1 = lax.dot_general(k1_ref[h], q1, nt, preferred_element_type=f32)
        s2 = lax.dot_general(k2_ref[h], q2, nt, preferred_element_type=f32)
        s1t, i1 = _topk_rows(s1, PEER_TOPK)
        s2t, i2 = _topk_rows(s2, PEER_TOPK)
        cand, cidx = _staircase_candidates(s1t, i1, s2t, i2)
        iota = lax.broadcasted_iota(i32, cand.shape, 0)
        sc, ex = [], []
        for _ in range(PEER_TOPK):
            m = jnp.max(cand, axis=0, keepdims=True)
            pos = jnp.min(jnp.where(cand == m, iota, cand.shape[0]), axis=0, keepdims=True)
            hit = iota == pos
            ex.append(jnp.sum(jnp.where(hit, cidx, 0), axis=0, keepdims=True))
            sc.append(m)
            cand = jnp.where(hit, -jnp.inf, cand)
        p = [jnp.exp(s - sc[0]) for s in sc]
        denom = p[0]
        for t in p[1:]:
            denom = denom + t
        e_rows += ex
        g_rows += [t / denom for t in p]
    ng = PEER_PAIRS // SUBLANES
    keys = [jnp.concatenate(e_rows[g * SUBLANES:(g + 1) * SUBLANES], axis=0) for g in range(ng)]
    vals = [jnp.concatenate(g_rows[g * SUBLANES:(g + 1) * SUBLANES], axis=0) for g in range(ng)]
    keys, vals = _bitonic_sort_groups(keys, vals)
    e_tok = jnp.concatenate(keys, axis=0).T
    e_ref[...] = e_tok
    gate_ref[...] = jnp.concatenate(vals, axis=0).T
    lane = lax.broadcasted_iota(i32, (Tt, LANES), 1)
    starts = jnp.zeros((Tt, LANES), i32)
    for nb, eloc_ref, off in ((nba, eloca_ref, 0), (nbv, elocv_ref, PEER_ST_STRIDE)):
        bsz = (PEER_NKEYS * PEER_NKEYS) // nb
        for b in range(nb):
            eloc_ref[b] = jnp.clip(e_tok - b * bsz, 0, bsz - 1)
        for b in range(1, nb + 1):
            cnt = jnp.sum((e_tok < b * bsz).astype(i32), axis=1, keepdims=True)
            starts = jnp.where(lane == off + b, cnt, starts)
    st_ref[...] = starts


def _route(q, k1, k2, nba, nbv, tt=128):
    T, QW = q.shape
    H, NK, half = k1.shape
    return pl.pallas_call(
        functools.partial(_route_kernel, nba, nbv),
        grid=(T // tt,),
        in_specs=[pl.BlockSpec((tt, QW), lambda i: (i, 0)),
                  pl.BlockSpec((H, NK, half), lambda i: (0, 0, 0)),
                  pl.BlockSpec((H, NK, half), lambda i: (0, 0, 0))],
        out_specs=[pl.BlockSpec((tt, PEER_PAIRS), lambda i: (i, 0)),
                   pl.BlockSpec((nba, tt, PEER_PAIRS), lambda i: (0, i, 0)),
                   pl.BlockSpec((nbv, tt, PEER_PAIRS), lambda i: (0, i, 0)),
                   pl.BlockSpec((tt, PEER_PAIRS), lambda i: (i, 0)),
                   pl.BlockSpec((tt, LANES), lambda i: (i, 0))],
        out_shape=[jax.ShapeDtypeStruct((T, PEER_PAIRS), i32),
                   jax.ShapeDtypeStruct((nba, T, PEER_PAIRS), i32),
                   jax.ShapeDtypeStruct((nbv, T, PEER_PAIRS), i32),
                   jax.ShapeDtypeStruct((T, PEER_PAIRS), f32),
                   jax.ShapeDtypeStruct((T, LANES), i32)],
        compiler_params=_cparams("parallel"),
    )(q, k1, k2)


PEER_ST_STRIDE = SUBLANES


def _chunk_range(st_smem, idx, chunk):
    s0 = st_smem[idx]
    s1 = st_smem[idx + 1]
    sh = int(math.log2(chunk))
    cs = lax.shift_right_logical(s0, sh)
    ce = jnp.where(s1 > s0, lax.shift_right_logical(s1 + (chunk - 1), sh), cs)
    return s0, s1, cs, ce


def _prefetch_lists(step, nsteps, copies_for):
    slot = step & 1

    @pl.when(step == 0)
    def _():
        for c in copies_for(step, slot):
            c.start()

    @pl.when(step + 1 < nsteps)
    def _():
        for c in copies_for(step + 1, 1 - slot):
            c.start()

    for c in copies_for(step, slot):
        c.wait()
    return slot


def _gelu_exact(x):
    return 0.5 * x * (1.0 + lax.erf(x * (2.0 ** -0.5)))


def _sublane_sums(vs, sub):
    def fold(a, b, s):
        ta = a + pltpu.roll(a, s, 0)
        tb = b + pltpu.roll(b, SUBLANES - s, 0)
        return jnp.where((sub & s) != 0, ta, tb)

    def node(rows, s):
        if len(rows) == 1:
            return vs[rows[0]]
        return fold(node([r for r in rows if r & s], 2 * s), node([r for r in rows if not r & s], 2 * s), s)

    return node(list(range(SUBLANES)), 1)


def _act_kernel(nba, nbv, eloc_hbm, st_hbm, h_ref, u_ref, e_ref, gate_ref, w_ref,
                e_smem, st_smem, stage_ref, act_ref, sem):
    i = pl.program_id(0)
    b = pl.program_id(1)
    Tk = h_ref.shape[0]
    T = Tk * pl.num_programs(0)
    n = Tk * PEER_PAIRS
    ns = Tk * PEER_ST_STRIDE
    lg = int(math.log2(nba))

    def copies_for(step, slot):
        si = lax.shift_right_logical(step, lg)
        sb = step & (nba - 1)
        src_e = eloc_hbm.at[pl.ds(pl.multiple_of((sb * T + si * Tk) * PEER_PAIRS, n), n)]
        src_s = st_hbm.at[pl.ds(pl.multiple_of(si * ns, ns), ns)]
        return (pltpu.make_async_copy(src_e, e_smem.at[pl.ds(pl.multiple_of(slot * n, n), n)], sem.at[slot, 0]),
                pltpu.make_async_copy(src_s, st_smem.at[pl.ds(pl.multiple_of(slot * ns, ns), ns)], sem.at[slot, 1]))

    slot = _prefetch_lists(i * nba + b, pl.num_programs(0) * nba, copies_for)
    eoff = slot * n
    soff = slot * ns + b
    sub = lax.broadcasted_iota(i32, (SUBLANES, LANES), 0)

    def token(t, carry):
        s0, s1, cs, ce = _chunk_range(st_smem, soff + t * PEER_ST_STRIDE, PEER_CHUNK)
        hrow = h_ref[t]
        base = eoff + t * PEER_PAIRS

        def flush(c, halves):
            first = pl.multiple_of(c * PEER_CHUNK, PEER_CHUNK)
            part = _sublane_sums(halves, sub)
            pos = sub + first
            pltpu.store(stage_ref.at[t, pl.ds(first, PEER_CHUNK), :], part,
                        mask=(pos >= s0) & (pos < s1))

        def chunk(c, pending):
            c_prev, halves_prev = pending
            first = c * PEER_CHUNK
            halves = []
            for k in range(PEER_CHUNK):
                pr = hrow * u_ref[e_smem[base + first + k]].astype(f32)
                halves.append(pr[:SUBLANES] + pr[SUBLANES:])
            flush(c_prev, halves_prev)
            return c, tuple(halves)

        zero = jnp.zeros((SUBLANES, LANES), f32)
        pending = (jnp.minimum(cs, PEER_NCHUNK - 1), (zero,) * PEER_CHUNK)
        flush(*lax.fori_loop(cs, ce, chunk, pending))
        return carry

    lax.fori_loop(0, Tk, token, 0)

    @pl.when(b == nba - 1)
    def _():
        unroll = 4

        def reduce_tokens(g, carry):
            for k in range(unroll):
                t = g * unroll + k
                act_ref[pl.ds(t, 1), :] = jnp.sum(stage_ref[t].T, axis=0, keepdims=True)
            return carry

        lax.fori_loop(0, Tk // unroll, reduce_tokens, 0)
        w = gate_ref[...] * _gelu_exact(act_ref[...])
        blk = lax.shift_right_logical(e_ref[...], int(math.log2(PEER_NKEYS * PEER_NKEYS // nbv)))
        for bb in range(nbv):
            w_ref[bb] = jnp.where(blk == bb, w, 0.0)


def _peer_act(eloc, starts, h3, u3, e_tok, gate, nba, nbv, tk):
    T = h3.shape[0]
    E = u3.shape[0]
    bsz = E // nba
    rowblk = h3.shape[1:]
    return pl.pallas_call(
        functools.partial(_act_kernel, nba, nbv),
        grid=(T // tk, nba),
        in_specs=[pl.BlockSpec(memory_space=pl.ANY),
                  pl.BlockSpec(memory_space=pl.ANY),
                  pl.BlockSpec((tk,) + rowblk, lambda i, b: (i, 0, 0)),
                  pl.BlockSpec((bsz,) + rowblk, lambda i, b: (b, 0, 0)),
                  pl.BlockSpec((tk, PEER_PAIRS), lambda i, b: (i, 0)),
                  pl.BlockSpec((tk, PEER_PAIRS), lambda i, b: (i, 0))],
        out_specs=pl.BlockSpec((nbv, tk, PEER_PAIRS), lambda i, b: (0, i, 0)),
        out_shape=jax.ShapeDtypeStruct((nbv, T, PEER_PAIRS), f32),
        scratch_shapes=[pltpu.SMEM((2 * tk * PEER_PAIRS,), i32),
                        pltpu.SMEM((2 * tk * PEER_ST_STRIDE,), i32),
                        pltpu.VMEM((tk, PEER_PAIRS, LANES), f32),
                        pltpu.VMEM((tk, PEER_PAIRS), f32),
                        pltpu.SemaphoreType.DMA((2, 2))],
        compiler_params=_cparams("arbitrary", "arbitrary"),
    )(eloc, starts, h3, u3, e_tok, gate)


PEER_VAL_CHUNK = 2 * PEER_CHUNK


def _val_kernel(eloc_hbm, st_hbm, w_hbm, v_hbm, o_ref, e_smem, st_smem, w_smem, v_ref, sem, vsem):
    b = pl.program_id(0)
    i = pl.program_id(1)
    ni = pl.num_programs(1)
    Tk = o_ref.shape[1]
    T = Tk * ni
    n = Tk * PEER_PAIRS
    ns = Tk * PEER_ST_STRIDE
    bsz = v_ref.shape[0]

    @pl.when(i == 0)
    def _():
        cp = pltpu.make_async_copy(v_hbm.at[pl.ds(b * bsz, bsz)], v_ref, vsem)
        cp.start()
        cp.wait()

    def copies_for(step, slot):
        sb = step // ni
        si = step - sb * ni
        lists = pl.ds(pl.multiple_of((sb * T + si * Tk) * PEER_PAIRS, n), n)
        dst = pl.ds(pl.multiple_of(slot * n, n), n)
        return (pltpu.make_async_copy(eloc_hbm.at[lists], e_smem.at[dst], sem.at[slot, 0]),
                pltpu.make_async_copy(st_hbm.at[pl.ds(pl.multiple_of(si * ns, ns), ns)],
                                      st_smem.at[pl.ds(pl.multiple_of(slot * ns, ns), ns)], sem.at[slot, 1]),
                pltpu.make_async_copy(w_hbm.at[lists], w_smem.at[dst], sem.at[slot, 2]))

    slot = _prefetch_lists(b * ni + i, pl.num_programs(0) * ni, copies_for)
    eoff = slot * n
    soff = slot * ns + b

    def token(t, carry):
        _, _, cs, ce = _chunk_range(st_smem, soff + t * PEER_ST_STRIDE, PEER_VAL_CHUNK)
        base = eoff + t * PEER_PAIRS

        def chunk(c, acc):
            first = base + c * PEER_VAL_CHUNK
            terms = [w_smem[first + k] * v_ref[e_smem[first + k]].astype(f32)
                     for k in range(PEER_VAL_CHUNK)]
            while len(terms) > 1:
                terms = [terms[j] + terms[j + 1] for j in range(0, len(terms), 2)]
            return acc + terms[0]

        o_ref[0, t] = lax.fori_loop(cs, ce, chunk, jnp.zeros(o_ref.shape[2:], f32))
        return carry

    lax.fori_loop(0, Tk, token, 0)


def _peer_val(eloc, starts, w, v3, nb, tk):
    T = eloc.shape[0] // (nb * PEER_PAIRS)
    E = v3.shape[0]
    bsz = E // nb
    rowblk = v3.shape[1:]
    return pl.pallas_call(
        _val_kernel,
        grid=(nb, T // tk),
        in_specs=[pl.BlockSpec(memory_space=pl.ANY),
                  pl.BlockSpec(memory_space=pl.ANY),
                  pl.BlockSpec(memory_space=pl.ANY),
                  pl.BlockSpec(memory_space=pl.ANY)],
        out_specs=pl.BlockSpec((1, tk) + rowblk, lambda b, i: (b, i, 0, 0)),
        out_shape=jax.ShapeDtypeStruct((nb, T) + rowblk, f32),
        scratch_shapes=[pltpu.SMEM((2 * tk * PEER_PAIRS,), i32),
                        pltpu.SMEM((2 * tk * PEER_ST_STRIDE,), i32),
                        pltpu.SMEM((2 * tk * PEER_PAIRS,), f32),
                        pltpu.VMEM((bsz,) + rowblk, v3.dtype),
                        pltpu.SemaphoreType.DMA((2, 3)),
                        pltpu.SemaphoreType.DMA(())],
        compiler_params=_cparams("arbitrary", "arbitrary"),
    )(eloc, starts, w, v3)


def _ln2_kernel(alpha, h_ref, f_ref, g_ref, b_ref, o_ref):
    o_ref[...] = _layer_norm_rows(alpha * h_ref[...] + jnp.sum(f_ref[...], axis=0), g_ref[...], b_ref[...])


def _ln2(h2, ffn, g, b, alpha, tm=512):
    T, D = h2.shape
    return pl.pallas_call(
        functools.partial(_ln2_kernel, alpha),
        grid=(T // tm,),
        in_specs=[pl.BlockSpec((tm, D), lambda i: (i, 0)),
                  pl.BlockSpec((ffn.shape[0], tm, D), lambda i: (0, i, 0)),
                  pl.BlockSpec((1, D), lambda i: (0, 0)),
                  pl.BlockSpec((1, D), lambda i: (0, 0))],
        out_specs=pl.BlockSpec((tm, D), lambda i: (i, 0)),
        out_shape=jax.ShapeDtypeStruct((T, D), f32),
        compiler_params=_cparams("parallel"),
    )(h2, ffn, g, b)


PEER_ACT_NBLOCKS = 4
PEER_VAL_NBLOCKS = 2
PEER_ACT_TOKENS = 128
PEER_VAL_TOKENS = 256


def _layer(x, w_in, w_a2, b_a, gla_norm_g, conv_w, conv_b, conv_ln_g, conv_ln_b, w_out,
           ln1_g, ln1_b, w_q, sub_k1, sub_k2, u_tab, v_tab, ln2_g, ln2_b, alpha):
    B, L, D = x.shape
    T = B * L
    kw = GLA_HEADS * GLA_DK
    vw = GLA_HEADS * GLA_DV
    split_r = 2 * kw + 2 * vw
    split_a = split_r + GLA_GATE_RANK
    cw = conv_w.shape[1]
    x2 = x.reshape(T, D)
    w_main = jnp.concatenate([w_in[:, :split_r], w_in[:, split_a:]], axis=1).astype(bf16)
    w_a = jnp.pad(w_in[:, split_r:split_a], ((0, 0), (0, LANES - GLA_GATE_RANK))).astype(bf16)
    w_a2p = jnp.pad(w_a2, ((0, LANES - GLA_GATE_RANK), (0, 0)))
    conv_wp = jnp.pad(conv_w, ((0, CONV_HALO - CONV_KERNEL), (0, 0)))
    proj, a_lr = _inproj(x2, w_main, w_a)
    proj3 = proj.reshape(B, L, -1)
    gla_out = _gla(proj3, a_lr.reshape(B, L, LANES), w_a2p, b_a.reshape(1, kw),
                   gla_norm_g.reshape(1, vw))
    conv_out = _conv(proj3, conv_wp, conv_b.reshape(1, cw), conv_ln_g.reshape(1, cw),
                     conv_ln_b.reshape(1, cw), val_blk=split_r // cw)
    h2 = _outproj(gla_out.reshape(T, vw), conv_out.reshape(T, cw), x2,
                  w_out[:vw].astype(bf16), w_out[vw:].astype(bf16),
                  ln1_g.reshape(1, D), ln1_b.reshape(1, D), alpha)
    q = _qproj(h2, w_q.astype(bf16))
    nba, nbv = PEER_ACT_NBLOCKS, PEER_VAL_NBLOCKS
    e_tok, eloc_a, eloc_v, gate, starts = _route(q, sub_k1.astype(bf16), sub_k2.astype(bf16), nba, nbv)
    rows = D // LANES
    u3 = u_tab.astype(bf16).reshape(-1, rows, LANES)
    v3 = v_tab.astype(bf16).reshape(-1, rows, LANES)
    h3 = h2.reshape(T, rows, LANES)
    starts_a = starts[:, :PEER_ST_STRIDE].reshape(-1)
    starts_v = starts[:, PEER_ST_STRIDE:2 * PEER_ST_STRIDE].reshape(-1)
    w = _peer_act(eloc_a.reshape(-1), starts_a, h3, u3, e_tok, gate, nba, nbv, PEER_ACT_TOKENS)
    ffn = _peer_val(eloc_v.reshape(-1), starts_v, w.reshape(-1), v3, nbv, PEER_VAL_TOKENS)
    return _ln2(h2, ffn.reshape(nbv, T, D), ln2_g.reshape(1, D), ln2_b.reshape(1, D), alpha).reshape(B, L, D)


def kernel(x, w_in, w_a2, b_a, gla_norm_g, conv_w, conv_b, conv_ln_g, conv_ln_b, w_out, ln1_g, ln1_b, w_q, sub_k1, sub_k2, u_tab, v_tab, ln2_g, ln2_b):
    depth = w_in.shape[0]
    alpha = (2.0 * depth) ** 0.25
    for l in range(depth):
        x = _layer(x, w_in[l], w_a2[l], b_a[l], gla_norm_g[l], conv_w[l], conv_b[l], conv_ln_g[l],
                   conv_ln_b[l], w_out[l], ln1_g[l], ln1_b[l], w_q[l], sub_k1[l], sub_k2[l],
                   u_tab[l], v_tab[l], ln2_g[l], ln2_b[l], alpha)
    return x
```

```python
import functools
import math

import jax
import jax.numpy as jnp
from jax import lax
from jax.experimental import pallas as pl
from jax.experimental.pallas import tpu as pltpu

f32 = jnp.float32
bf16 = jnp.bfloat16
i32 = jnp.int32

LANES = 128
SUBLANES = 8
VMEM_LIMIT_BYTES = 56 * 1024 * 1024

GLA_HEADS = 4
GLA_DK = 128
GLA_DV = 256
GLA_GATE_RANK = 16
GLA_TAU = 16.0
GLA_CHUNK = 64
CONV_KERNEL = 31
CONV_HALO = 32
PEER_HEADS = 8
PEER_NKEYS = 128
PEER_TOPK = 16
PEER_PAIRS = PEER_HEADS * PEER_TOPK
PEER_CHUNK = SUBLANES
PEER_NCHUNK = PEER_PAIRS // PEER_CHUNK
LN_EPS = 1e-5


def _cparams(*sem):
    return pltpu.CompilerParams(dimension_semantics=sem, vmem_limit_bytes=VMEM_LIMIT_BYTES)


def _layer_norm_rows(x, g, b=None):
    mu = jnp.mean(x, axis=-1, keepdims=True)
    xc = x - mu
    var = jnp.mean(xc * xc, axis=-1, keepdims=True)
    out = xc * lax.rsqrt(var + LN_EPS) * g
    if b is not None:
        out = out + b
    return out


def _silu(x):
    return x * jax.nn.sigmoid(x)


def _inproj_kernel(x_ref, w_ref, wa_ref, o_ref, a_ref, xb_ref):
    @pl.when(pl.program_id(1) == 0)
    def _():
        xb = x_ref[...].astype(bf16)
        xb_ref[...] = xb
        a_ref[...] = jnp.dot(xb, wa_ref[...], preferred_element_type=f32)

    o_ref[...] = jnp.dot(xb_ref[...], w_ref[...], preferred_element_type=f32)


def _inproj(x2, w_main, w_a, tm=512, tn=1024):
    T, D = x2.shape
    N = w_main.shape[1]
    return pl.pallas_call(
        _inproj_kernel,
        grid=(T // tm, N // tn),
        in_specs=[pl.BlockSpec((tm, D), lambda i, j: (i, 0)),
                  pl.BlockSpec((D, tn), lambda i, j: (0, j)),
                  pl.BlockSpec((D, LANES), lambda i, j: (0, 0))],
        out_specs=[pl.BlockSpec((tm, tn), lambda i, j: (i, j)),
                   pl.BlockSpec((tm, LANES), lambda i, j: (i, 0))],
        out_shape=[jax.ShapeDtypeStruct((T, N), f32), jax.ShapeDtypeStruct((T, LANES), f32)],
        scratch_shapes=[pltpu.VMEM((tm, D), bf16)],
        compiler_params=_cparams("parallel", "arbitrary"),
    )(x2, w_main, w_a)


def _gla_kernel(q_ref, k_ref, v_ref, r_ref, a_ref, wa2_ref, ba_ref, g_ref, o_ref, st_ref):
    C = GLA_CHUNK
    L = q_ref.shape[1]
    st_ref[...] = jnp.zeros_like(st_ref)
    row = lax.broadcasted_iota(i32, (C, C), 0)
    col = lax.broadcasted_iota(i32, (C, C), 1)
    causal = row >= col
    tril = causal.astype(f32)
    scale = GLA_DK ** -0.5
    DK, DV = GLA_DK, GLA_DV
    heads = q_ref.shape[2] // DK
    nt = (((1,), (1,)), ((), ()))

    def chunk(n, carry):
        rows = pl.ds(pl.multiple_of(n * C, C), C)
        a_lr = a_ref[0, rows, :]
        for hh in range(heads):
            kc = slice(hh * DK, (hh + 1) * DK)
            vc = slice(hh * DV, (hh + 1) * DV)
            z = jnp.dot(a_lr, wa2_ref[:, kc], preferred_element_type=f32,
                        precision=lax.Precision.HIGHEST) + ba_ref[:, kc]
            log_a = (jnp.minimum(z, 0.0) - jnp.log1p(jnp.exp(-jnp.abs(z)))) / GLA_TAU
            b = jnp.dot(tril, log_a, preferred_element_type=f32, precision=lax.Precision.HIGHEST)
            b_ref = b[C // 2:C // 2 + 1, :]
            b_last = b[C - 1:C, :]
            q = q_ref[0, rows, kc] * scale
            k = k_ref[0, rows, kc]
            v = v_ref[0, rows, vc].astype(bf16)
            q_in = (q * jnp.exp(b - b_ref)).astype(bf16)
            k_in = (k * jnp.exp(b_ref - b)).astype(bf16)
            scores = lax.dot_general(q_in, k_in, nt, preferred_element_type=f32)
            scores = jnp.where(causal, scores, 0.0)
            o = jnp.dot(scores.astype(bf16), v, preferred_element_type=f32)
            st = st_ref[hh]
            q_dec = (q * jnp.exp(b)).astype(bf16)
            o = o + lax.dot_general(q_dec, st.astype(bf16), nt, preferred_element_type=f32)
            k_dec = (k * jnp.exp(b_last - b)).astype(bf16)
            kv_t = lax.dot_general(v, k_dec, (((0,), (0,)), ((), ())), preferred_element_type=f32)
            st_ref[hh] = st * jnp.exp(b_last) + kv_t
            o = _layer_norm_rows(o, g_ref[:, vc])
            o_ref[0, rows, vc] = (o * _silu(r_ref[0, rows, vc])).astype(o_ref.dtype)
        return carry

    lax.fori_loop(0, L // C, chunk, 0)


def _gla(proj3, a3, w_a2p, b_a, gla_g, heads_per_step=2):
    B, L, _ = proj3.shape
    H, hs = GLA_HEADS, heads_per_step
    DK, DV = hs * GLA_DK, hs * GLA_DV
    kq = H // hs
    kv = 2 * kq * DK // DV
    kr = kv + H // hs
    return pl.pallas_call(
        _gla_kernel,
        grid=(B, H // hs),
        in_specs=[pl.BlockSpec((1, L, DK), lambda b, h: (b, 0, h)),
                  pl.BlockSpec((1, L, DK), lambda b, h: (b, 0, kq + h)),
                  pl.BlockSpec((1, L, DV), lambda b, h: (b, 0, kv + h)),
                  pl.BlockSpec((1, L, DV), lambda b, h: (b, 0, kr + h)),
                  pl.BlockSpec((1, L, LANES), lambda b, h: (b, 0, 0)),
                  pl.BlockSpec((LANES, DK), lambda b, h: (0, h)),
                  pl.BlockSpec((1, DK), lambda b, h: (0, h)),
                  pl.BlockSpec((1, DV), lambda b, h: (0, h))],
        out_specs=pl.BlockSpec((1, L, DV), lambda b, h: (b, 0, h)),
        out_shape=jax.ShapeDtypeStruct((B, L, H * GLA_DV), bf16),
        scratch_shapes=[pltpu.VMEM((hs, GLA_DV, GLA_DK), f32)],
        compiler_params=_cparams("parallel", "parallel"),
    )(proj3, proj3, proj3, proj3, a3, w_a2p, b_a, gla_g)


def _conv_kernel(val_ref, gate_ref, w_ref, cb_ref, g_ref, b_ref, o_ref, glu_ref, acc_ref):
    tl = val_ref.shape[1]
    W = val_ref.shape[2]
    rc = 64

    @pl.when(pl.program_id(1) == 0)
    def _():
        glu_ref[0:CONV_HALO, :] = jnp.zeros((CONV_HALO, W), f32)

    @pl.when(pl.program_id(1) != 0)
    def _():
        glu_ref[0:CONV_HALO, :] = glu_ref[tl:tl + CONV_HALO, :]

    glu_ref[CONV_HALO:CONV_HALO + tl, :] = val_ref[0] * jax.nn.sigmoid(gate_ref[0])
    lead = CONV_HALO - (CONV_KERNEL - 1)

    def strip(s, carry):
        cols = pl.ds(pl.multiple_of(s * LANES, LANES), LANES)
        wj = w_ref[:, cols]
        bias = cb_ref[:, cols]
        for r0 in range(0, tl, rc):
            acc = jnp.zeros((rc, LANES), f32) + bias
            for j in range(CONV_KERNEL):
                acc = acc + glu_ref[pl.ds(r0 + lead + j, rc), cols] * wj[j:j + 1, :]
            acc_ref[pl.ds(r0, rc), cols] = acc
        return carry

    lax.fori_loop(0, W // LANES, strip, 0)
    y = _layer_norm_rows(acc_ref[...], g_ref[...], b_ref[...])
    o_ref[0] = _silu(y).astype(o_ref.dtype)


def _conv(proj3, conv_wp, conv_b, ln_g, ln_b, val_blk, tl=256):
    B, L, _ = proj3.shape
    W = conv_wp.shape[1]
    return pl.pallas_call(
        _conv_kernel,
        grid=(B, L // tl),
        in_specs=[pl.BlockSpec((1, tl, W), lambda b, l: (b, l, val_blk)),
                  pl.BlockSpec((1, tl, W), lambda b, l: (b, l, val_blk + 1)),
                  pl.BlockSpec(conv_wp.shape, lambda b, l: (0, 0)),
                  pl.BlockSpec((1, W), lambda b, l: (0, 0)),
                  pl.BlockSpec((1, W), lambda b, l: (0, 0)),
                  pl.BlockSpec((1, W), lambda b, l: (0, 0))],
        out_specs=pl.BlockSpec((1, tl, W), lambda b, l: (b, l, 0)),
        out_shape=jax.ShapeDtypeStruct((B, L, W), bf16),
        scratch_shapes=[pltpu.VMEM((tl + CONV_HALO, W), f32), pltpu.VMEM((tl, W), f32)],
        compiler_params=_cparams("parallel", "arbitrary"),
    )(proj3, proj3, conv_wp, conv_b, ln_g, ln_b)


def _outproj_kernel(alpha, ga_ref, cv_ref, x_ref, w1_ref, w2_ref, g_ref, b_ref, h_ref):
    mix = jnp.dot(ga_ref[...], w1_ref[...], preferred_element_type=f32)
    mix = mix + jnp.dot(cv_ref[...], w2_ref[...], preferred_element_type=f32)
    h_ref[...] = _layer_norm_rows(alpha * x_ref[...] + mix, g_ref[...], b_ref[...])


def _outproj(gla_out, conv_out, x2, w_o1, w_o2, g, b, alpha, tm=256):
    T, D = x2.shape
    W1 = gla_out.shape[1]
    W2 = conv_out.shape[1]
    return pl.pallas_call(
        functools.partial(_outproj_kernel, alpha),
        grid=(T // tm,),
        in_specs=[pl.BlockSpec((tm, W1), lambda i: (i, 0)),
                  pl.BlockSpec((tm, W2), lambda i: (i, 0)),
                  pl.BlockSpec((tm, D), lambda i: (i, 0)),
                  pl.BlockSpec((W1, D), lambda i: (0, 0)),
                  pl.BlockSpec((W2, D), lambda i: (0, 0)),
                  pl.BlockSpec((1, D), lambda i: (0, 0)),
                  pl.BlockSpec((1, D), lambda i: (0, 0))],
        out_specs=pl.BlockSpec((tm, D), lambda i: (i, 0)),
        out_shape=jax.ShapeDtypeStruct((T, D), f32),
        compiler_params=_cparams("parallel"),
    )(gla_out, conv_out, x2, w_o1, w_o2, g, b)


def _qproj_kernel(h_ref, w_ref, q_ref):
    q_ref[...] = jnp.dot(h_ref[...].astype(bf16), w_ref[...],
                         preferred_element_type=f32).astype(q_ref.dtype)


def _qproj(h2, w_q, tm=256):
    T, D = h2.shape
    N = w_q.shape[1]
    return pl.pallas_call(
        _qproj_kernel,
        grid=(T // tm,),
        in_specs=[pl.BlockSpec((tm, D), lambda i: (i, 0)),
                  pl.BlockSpec((D, N), lambda i: (0, 0))],
        out_specs=pl.BlockSpec((tm, N), lambda i: (i, 0)),
        out_shape=jax.ShapeDtypeStruct((T, N), bf16),
        compiler_params=_cparams("parallel"),
    )(h2, w_q)


def _topk_rows(s, k):
    R = s.shape[0]
    iota = lax.broadcasted_iota(i32, s.shape, 0)
    vals, idxs = [], []
    for _ in range(k):
        m = jnp.max(s, axis=0, keepdims=True)
        idx = jnp.min(jnp.where(s == m, iota, R), axis=0, keepdims=True)
        vals.append(m)
        idxs.append(idx)
        s = jnp.where(iota == idx, -jnp.inf, s)
    return vals, idxs


def _staircase_candidates(s1t, i1, s2t, i2):
    K = PEER_TOPK
    s1c, i1c = jnp.concatenate(s1t, axis=0), jnp.concatenate(i1, axis=0)
    s2c, i2c = jnp.concatenate(s2t, axis=0), jnp.concatenate(i2, axis=0)
    sub = lax.broadcasted_iota(i32, (SUBLANES, s1c.shape[1]), 0)
    vals, idxs = [], []
    a = 0
    while K // (a + 1) > 1:
        n_b = K // (a + 1)
        for b0 in range(0, n_b, SUBLANES):
            v = s1t[a] + s2c[b0:b0 + SUBLANES]
            if n_b - b0 < SUBLANES:
                v = jnp.where(sub < n_b - b0, v, -jnp.inf)
            vals.append(v)
            idxs.append(i1[a] * PEER_NKEYS + i2c[b0:b0 + SUBLANES])
        a += 1
    assert K - a == SUBLANES
    vals.append(s1c[a:K] + s2t[0])
    idxs.append(i1c[a:K] * PEER_NKEYS + i2[0])
    return jnp.concatenate(vals, axis=0), jnp.concatenate(idxs, axis=0)


def _bitonic_sort_groups(keys, vals):
    ng = len(keys)
    n = ng * SUBLANES
    sub = lax.broadcasted_iota(i32, keys[0].shape, 0)
    k = 2
    while k <= n:
        j = k // 2
        while j >= 1:
            if j >= SUBLANES:
                gj = j // SUBLANES
                for g in range(ng):
                    if g & gj:
                        continue
                    p = g | gj
                    asc = ((g * SUBLANES) & k) == 0
                    swap = keys[g] > keys[p] if asc else keys[g] < keys[p]
                    kg = jnp.where(swap, keys[p], keys[g])
                    kp = jnp.where(swap, keys[g], keys[p])
                    vg = jnp.where(swap, vals[p], vals[g])
                    vp = jnp.where(swap, vals[g], vals[p])
                    keys[g], keys[p], vals[g], vals[p] = kg, kp, vg, vp
            else:
                low = (sub & j) == 0
                for g in range(ng):
                    x, y = keys[g], vals[g]
                    px = jnp.where(low, pltpu.roll(x, SUBLANES - j, 0), pltpu.roll(x, j, 0))
                    py = jnp.where(low, pltpu.roll(y, SUBLANES - j, 0), pltpu.roll(y, j, 0))
                    if k >= SUBLANES:
                        asc = ((g * SUBLANES) & k) == 0
                        take_min = low if asc else jnp.logical_not(low)
                    else:
                        take_min = low == ((sub & k) == 0)
                    sel = (take_min & (px < x)) | (jnp.logical_not(take_min) & (px > x))
                    keys[g] = jnp.where(sel, px, x)
                    vals[g] = jnp.where(sel, py, y)
            j //= 2
        k *= 2
    return keys, vals


def _route_kernel(nba, nbv, q_ref, k1_ref, k2_ref, e_ref, eloca_ref, elocv_ref, gate_ref, st_ref):
    Tt = q_ref.shape[0]
    half = q_ref.shape[1] // PEER_HEADS // 2
    nt = (((1,), (1,)), ((), ()))
    e_rows, g_rows = [], []
    for h in range(PEER_HEADS):
        q1 = q_ref[:, (2 * h) * half:(2 * h + 1) * half]
        q2 = q_ref[:, (2 * h + 1) * half:(2 * h + 2) * half]
        s1 = lax.dot_general(k1_ref[h], q1, nt, preferred_element_type=f32)
        s2 = lax.dot_general(k2_ref[h], q2, nt, preferred_element_type=f32)
        s1t, i1 = _topk_rows(s1, PEER_TOPK)
        s2t, i2 = _topk_rows(s2, PEER_TOPK)
        cand, cidx = _staircase_candidates(s1t, i1, s2t, i2)
        iota = lax.broadcasted_iota(i32, cand.shape, 0)
        sc, ex = [], []
        for _ in range(PEER_TOPK):
            m = jnp.max(cand, axis=0, keepdims=True)
            pos = jnp.min(jnp.where(cand == m, iota, cand.shape[0]), axis=0, keepdims=True)
            hit = iota == pos
            ex.append(jnp.sum(jnp.where(hit, cidx, 0), axis=0, keepdims=True))
            sc.append(m)
            cand = jnp.where(hit, -jnp.inf, cand)
        p = [jnp.exp(s - sc[0]) for s in sc]
        denom = p[0]
        for t in p[1:]:
            denom = denom + t
        e_rows += ex
        g_rows += [t / denom for t in p]
    ng = PEER_PAIRS // SUBLANES
    keys = [jnp.concatenate(e_rows[g * SUBLANES:(g + 1) * SUBLANES], axis=0) for g in range(ng)]
    vals = [jnp.concatenate(g_rows[g * SUBLANES:(g + 1) * SUBLANES], axis=0) for g in range(ng)]
    keys, vals = _bitonic_sort_groups(keys, vals)
    e_tok = jnp.concatenate(keys, axis=0).T
    e_ref[...] = e_tok
    gate_ref[...] = jnp.concatenate(vals, axis=0).T
    lane = lax.broadcasted_iota(i32, (Tt, LANES), 1)
    starts = jnp.zeros((Tt, LANES), i32)
    for nb, eloc_ref, off in ((nba, eloca_ref, 0), (nbv, elocv_ref, PEER_ST_STRIDE)):
        bsz = (PEER_NKEYS * PEER_NKEYS) // nb
        for b in range(nb):
            eloc_ref[b] = jnp.clip(e_tok - b * bsz, 0, bsz - 1)
        for b in range(1, nb + 1):
            cnt = jnp.sum((e_tok < b * bsz).astype(i32), axis=1, keepdims=True)
            starts = jnp.where(lane == off + b, cnt, starts)
    st_ref[...] = starts


def _route(q, k1, k2, nba, nbv, tt=128):
    T, QW = q.shape
    H, NK, half = k1.shape
    return pl.pallas_call(
        functools.partial(_route_kernel, nba, nbv),
        grid=(T // tt,),
        in_specs=[pl.BlockSpec((tt, QW), lambda i: (i, 0)),
                  pl.BlockSpec((H, NK, half), lambda i: (0, 0, 0)),
                  pl.BlockSpec((H, NK, half), lambda i: (0, 0, 0))],
        out_specs=[pl.BlockSpec((tt, PEER_PAIRS), lambda i: (i, 0)),
                   pl.BlockSpec((nba, tt, PEER_PAIRS), lambda i: (0, i, 0)),
                   pl.BlockSpec((nbv, tt, PEER_PAIRS), lambda i: (0, i, 0)),
                   pl.BlockSpec((tt, PEER_PAIRS), lambda i: (i, 0)),
                   pl.BlockSpec((tt, LANES), lambda i: (i, 0))],
        out_shape=[jax.ShapeDtypeStruct((T, PEER_PAIRS), i32),
                   jax.ShapeDtypeStruct((nba, T, PEER_PAIRS), i32),
                   jax.ShapeDtypeStruct((nbv, T, PEER_PAIRS), i32),
                   jax.ShapeDtypeStruct((T, PEER_PAIRS), f32),
                   jax.ShapeDtypeStruct((T, LANES), i32)],
        compiler_params=_cparams("parallel"),
    )(q, k1, k2)


PEER_ST_STRIDE = SUBLANES


def _chunk_range(st_smem, idx, chunk):
    s0 = st_smem[idx]
    s1 = st_smem[idx + 1]
    sh = int(math.log2(chunk))
    cs = lax.shift_right_logical(s0, sh)
    ce = jnp.where(s1 > s0, lax.shift_right_logical(s1 + (chunk - 1), sh), cs)
    return s0, s1, cs, ce


def _prefetch_lists(step, nsteps, copies_for):
    slot = step & 1

    @pl.when(step == 0)
    def _():
        for c in copies_for(step, slot):
            c.start()

    @pl.when(step + 1 < nsteps)
    def _():
        for c in copies_for(step + 1, 1 - slot):
            c.start()

    for c in copies_for(step, slot):
        c.wait()
    return slot


def _gelu_exact(x):
    return 0.5 * x * (1.0 + lax.erf(x * (2.0 ** -0.5)))


def _fold(a, b, s, sub):
    ta = a + pltpu.roll(a, s, 0)
    tb = b + pltpu.roll(b, SUBLANES - s, 0)
    return jnp.where((sub & s) != 0, ta, tb)


def _fold_first(vs, sub):
    return tuple(_fold(vs[r + 4], vs[r], 4, sub) for r in range(4))


def _fold_rest(ms, sub):
    return _fold(_fold(ms[3], ms[1], 2, sub), _fold(ms[2], ms[0], 2, sub), 1, sub)


def _act_kernel(nba, nbv, eloc_hbm, st_hbm, h_ref, u_ref, e_ref, gate_ref, w_ref,
                e_smem, st_smem, stage_ref, act_ref, sem):
    i = pl.program_id(0)
    b = pl.program_id(1)
    Tk = h_ref.shape[0]
    T = Tk * pl.num_programs(0)
    n = Tk * PEER_PAIRS
    ns = Tk * PEER_ST_STRIDE
    lg = int(math.log2(nba))

    def copies_for(step, slot):
        si = lax.shift_right_logical(step, lg)
        sb = step & (nba - 1)
        src_e = eloc_hbm.at[pl.ds(pl.multiple_of((sb * T + si * Tk) * PEER_PAIRS, n), n)]
        src_s = st_hbm.at[pl.ds(pl.multiple_of(si * ns, ns), ns)]
        return (pltpu.make_async_copy(src_e, e_smem.at[pl.ds(pl.multiple_of(slot * n, n), n)], sem.at[slot, 0]),
                pltpu.make_async_copy(src_s, st_smem.at[pl.ds(pl.multiple_of(slot * ns, ns), ns)], sem.at[slot, 1]))

    slot = _prefetch_lists(i * nba + b, pl.num_programs(0) * nba, copies_for)
    eoff = slot * n
    soff = slot * ns + b
    sub = lax.broadcasted_iota(i32, (SUBLANES, LANES), 0)

    def token(t, carry):
        s0, s1, cs, ce = _chunk_range(st_smem, soff + t * PEER_ST_STRIDE, PEER_CHUNK)
        hrow = h_ref[t]
        base = eoff + t * PEER_PAIRS

        def flush(c, folded):
            first = pl.multiple_of(c * PEER_CHUNK, PEER_CHUNK)
            part = _fold_rest(folded, sub)
            pos = sub + first
            pltpu.store(stage_ref.at[t, pl.ds(first, PEER_CHUNK), :], part,
                        mask=(pos >= s0) & (pos < s1))

        def expert_ids(c):
            first = base + jnp.minimum(c, PEER_NCHUNK - 1) * PEER_CHUNK
            return tuple(e_smem[first + k] for k in range(PEER_CHUNK))

        def chunk(c, pending):
            c_prev, folded_prev, ids = pending
            halves = []
            for k in range(PEER_CHUNK):
                pr = hrow * u_ref[ids[k]].astype(f32)
                halves.append(pr[:SUBLANES] + pr[SUBLANES:])
            flush(c_prev, folded_prev)
            return c, _fold_first(halves, sub), expert_ids(c + 1)

        zero = jnp.zeros((SUBLANES, LANES), f32)
        pending = (jnp.minimum(cs, PEER_NCHUNK - 1), (zero,) * (PEER_CHUNK // 2), expert_ids(cs))
        c_last, folded_last, _ = lax.fori_loop(cs, ce, chunk, pending)
        flush(c_last, folded_last)
        return carry

    lax.fori_loop(0, Tk, token, 0)

    @pl.when(b == nba - 1)
    def _():
        unroll = 4

        def reduce_tokens(g, carry):
            for k in range(unroll):
                t = g * unroll + k
                act_ref[pl.ds(t, 1), :] = jnp.sum(stage_ref[t].T, axis=0, keepdims=True)
            return carry

        lax.fori_loop(0, Tk // unroll, reduce_tokens, 0)
        w = gate_ref[...] * _gelu_exact(act_ref[...])
        blk = lax.shift_right_logical(e_ref[...], int(math.log2(PEER_NKEYS * PEER_NKEYS // nbv)))
        for bb in range(nbv):
            w_ref[bb] = jnp.where(blk == bb, w, 0.0)


def _peer_act(eloc, starts, h3, u3, e_tok, gate, nba, nbv, tk):
    T = h3.shape[0]
    E = u3.shape[0]
    bsz = E // nba
    rowblk = h3.shape[1:]
    return pl.pallas_call(
        functools.partial(_act_kernel, nba, nbv),
        grid=(T // tk, nba),
        in_specs=[pl.BlockSpec(memory_space=pl.ANY),
                  pl.BlockSpec(memory_space=pl.ANY),
                  pl.BlockSpec((tk,) + rowblk, lambda i, b: (i, 0, 0)),
                  pl.BlockSpec((bsz,) + rowblk, lambda i, b: (b, 0, 0)),
                  pl.BlockSpec((tk, PEER_PAIRS), lambda i, b: (i, 0)),
                  pl.BlockSpec((tk, PEER_PAIRS), lambda i, b: (i, 0))],
        out_specs=pl.BlockSpec((nbv, tk, PEER_PAIRS), lambda i, b: (0, i, 0)),
        out_shape=jax.ShapeDtypeStruct((nbv, T, PEER_PAIRS), f32),
        scratch_shapes=[pltpu.SMEM((2 * tk * PEER_PAIRS,), i32),
                        pltpu.SMEM((2 * tk * PEER_ST_STRIDE,), i32),
                        pltpu.VMEM((tk, PEER_PAIRS, LANES), f32),
                        pltpu.VMEM((tk, PEER_PAIRS), f32),
                        pltpu.SemaphoreType.DMA((2, 2))],
        compiler_params=_cparams("arbitrary", "arbitrary"),
    )(eloc, starts, h3, u3, e_tok, gate)


PEER_VAL_CHUNK = 4 * PEER_CHUNK
PEER_VAL_NACC = 4


def _val_kernel(eloc_hbm, st_hbm, w_hbm, v_hbm, o_ref, e_smem, st_smem, w_smem, v_ref, sem, vsem):
    b = pl.program_id(0)
    i = pl.program_id(1)
    ni = pl.num_programs(1)
    Tk = o_ref.shape[1]
    T = Tk * ni
    n = Tk * PEER_PAIRS
    ns = Tk * PEER_ST_STRIDE
    bsz = v_ref.shape[0]

    @pl.when(i == 0)
    def _():
        cp = pltpu.make_async_copy(v_hbm.at[pl.ds(b * bsz, bsz)], v_ref, vsem)
        cp.start()
        cp.wait()

    def copies_for(step, slot):
        sb = step // ni
        si = step - sb * ni
        lists = pl.ds(pl.multiple_of((sb * T + si * Tk) * PEER_PAIRS, n), n)
        dst = pl.ds(pl.multiple_of(slot * n, n), n)
        return (pltpu.make_async_copy(eloc_hbm.at[lists], e_smem.at[dst], sem.at[slot, 0]),
                pltpu.make_async_copy(st_hbm.at[pl.ds(pl.multiple_of(si * ns, ns), ns)],
                                      st_smem.at[pl.ds(pl.multiple_of(slot * ns, ns), ns)], sem.at[slot, 1]),
                pltpu.make_async_copy(w_hbm.at[lists], w_smem.at[dst], sem.at[slot, 2]))

    slot = _prefetch_lists(b * ni + i, pl.num_programs(0) * ni, copies_for)
    eoff = slot * n
    soff = slot * ns + b

    def token(t, carry):
        _, _, cs, ce = _chunk_range(st_smem, soff + t * PEER_ST_STRIDE, PEER_VAL_CHUNK)
        base = eoff + t * PEER_PAIRS

        def chunk(c, accs):
            first = base + c * PEER_VAL_CHUNK
            accs = list(accs)
            for k in range(PEER_VAL_CHUNK):
                term = w_smem[first + k] * v_ref[e_smem[first + k]].astype(f32)
                accs[k % PEER_VAL_NACC] = accs[k % PEER_VAL_NACC] + term
            return tuple(accs)

        zero = jnp.zeros(o_ref.shape[2:], f32)
        accs = lax.fori_loop(cs, ce, chunk, (zero,) * PEER_VAL_NACC)
        o_ref[0, t] = (accs[0] + accs[1]) + (accs[2] + accs[3])
        return carry

    lax.fori_loop(0, Tk, token, 0)


def _peer_val(eloc, starts, w, v3, nb, tk):
    T = eloc.shape[0] // (nb * PEER_PAIRS)
    E = v3.shape[0]
    bsz = E // nb
    rowblk = v3.shape[1:]
    return pl.pallas_call(
        _val_kernel,
        grid=(nb, T // tk),
        in_specs=[pl.BlockSpec(memory_space=pl.ANY),
                  pl.BlockSpec(memory_space=pl.ANY),
                  pl.BlockSpec(memory_space=pl.ANY),
                  pl.BlockSpec(memory_space=pl.ANY)],
        out_specs=pl.BlockSpec((1, tk) + rowblk, lambda b, i: (b, i, 0, 0)),
        out_shape=jax.ShapeDtypeStruct((nb, T) + rowblk, f32),
        scratch_shapes=[pltpu.SMEM((2 * tk * PEER_PAIRS,), i32),
                        pltpu.SMEM((2 * tk * PEER_ST_STRIDE,), i32),
                        pltpu.SMEM((2 * tk * PEER_PAIRS,), f32),
                        pltpu.VMEM((bsz,) + rowblk, v3.dtype),
                        pltpu.SemaphoreType.DMA((2, 3)),
                        pltpu.SemaphoreType.DMA(())],
        compiler_params=_cparams("arbitrary", "arbitrary"),
    )(eloc, starts, w, v3)


def _ln2_kernel(alpha, h_ref, f_ref, g_ref, b_ref, o_ref):
    o_ref[...] = _layer_norm_rows(alpha * h_ref[...] + jnp.sum(f_ref[...], axis=0), g_ref[...], b_ref[...])


def _ln2(h2, ffn, g, b, alpha, tm=512):
    T, D = h2.shape
    return pl.pallas_call(
        functools.partial(_ln2_kernel, alpha),
        grid=(T // tm,),
        in_specs=[pl.BlockSpec((tm, D), lambda i: (i, 0)),
                  pl.BlockSpec((ffn.shape[0], tm, D), lambda i: (0, i, 0)),
                  pl.BlockSpec((1, D), lambda i: (0, 0)),
                  pl.BlockSpec((1, D), lambda i: (0, 0))],
        out_specs=pl.BlockSpec((tm, D), lambda i: (i, 0)),
        out_shape=jax.ShapeDtypeStruct((T, D), f32),
        compiler_params=_cparams("parallel"),
    )(h2, ffn, g, b)


PEER_ACT_NBLOCKS = 4
PEER_VAL_NBLOCKS = 2
PEER_ACT_TOKENS = 128
PEER_VAL_TOKENS = 256


def _layer(x, w_in, w_a2, b_a, gla_norm_g, conv_w, conv_b, conv_ln_g, conv_ln_b, w_out,
           ln1_g, ln1_b, w_q, sub_k1, sub_k2, u_tab, v_tab, ln2_g, ln2_b, alpha):
    B, L, D = x.shape
    T = B * L
    kw = GLA_HEADS * GLA_DK
    vw = GLA_HEADS * GLA_DV
    split_r = 2 * kw + 2 * vw
    split_a = split_r + GLA_GATE_RANK
    cw = conv_w.shape[1]
    x2 = x.reshape(T, D)
    w_main = jnp.concatenate([w_in[:, :split_r], w_in[:, split_a:]], axis=1).astype(bf16)
    w_a = jnp.pad(w_in[:, split_r:split_a], ((0, 0), (0, LANES - GLA_GATE_RANK))).astype(bf16)
    w_a2p = jnp.pad(w_a2, ((0, LANES - GLA_GATE_RANK), (0, 0)))
    conv_wp = jnp.pad(conv_w, ((0, CONV_HALO - CONV_KERNEL), (0, 0)))
    proj, a_lr = _inproj(x2, w_main, w_a)
    proj3 = proj.reshape(B, L, -1)
    gla_out = _gla(proj3, a_lr.reshape(B, L, LANES), w_a2p, b_a.reshape(1, kw),
                   gla_norm_g.reshape(1, vw))
    conv_out = _conv(proj3, conv_wp, conv_b.reshape(1, cw), conv_ln_g.reshape(1, cw),
                     conv_ln_b.reshape(1, cw), val_blk=split_r // cw)
    h2 = _outproj(gla_out.reshape(T, vw), conv_out.reshape(T, cw), x2,
                  w_out[:vw].astype(bf16), w_out[vw:].astype(bf16),
                  ln1_g.reshape(1, D), ln1_b.reshape(1, D), alpha)
    q = _qproj(h2, w_q.astype(bf16))
    nba, nbv = PEER_ACT_NBLOCKS, PEER_VAL_NBLOCKS
    e_tok, eloc_a, eloc_v, gate, starts = _route(q, sub_k1.astype(bf16), sub_k2.astype(bf16), nba, nbv)
    rows = D // LANES
    u3 = u_tab.astype(bf16).reshape(-1, rows, LANES)
    v3 = v_tab.astype(bf16).reshape(-1, rows, LANES)
    h3 = h2.reshape(T, rows, LANES)
    starts_a = starts[:, :PEER_ST_STRIDE].reshape(-1)
    starts_v = starts[:, PEER_ST_STRIDE:2 * PEER_ST_STRIDE].reshape(-1)
    w = _peer_act(eloc_a.reshape(-1), starts_a, h3, u3, e_tok, gate, nba, nbv, PEER_ACT_TOKENS)
    ffn = _peer_val(eloc_v.reshape(-1), starts_v, w.reshape(-1), v3, nbv, PEER_VAL_TOKENS)
    return _ln2(h2, ffn.reshape(nbv, T, D), ln2_g.reshape(1, D), ln2_b.reshape(1, D), alpha).reshape(B, L, D)


def kernel(x, w_in, w_a2, b_a, gla_norm_g, conv_w, conv_b, conv_ln_g, conv_ln_b, w_out, ln1_g, ln1_b, w_q, sub_k1, sub_k2, u_tab, v_tab, ln2_g, ln2_b):
    depth = w_in.shape[0]
    alpha = (2.0 * depth) ** 0.25
    for l in range(depth):
        x = _layer(x, w_in[l], w_a2[l], b_a[l], gla_norm_g[l], conv_w[l], conv_b[l], conv_ln_g[l],
                   conv_ln_b[l], w_out[l], ln1_g[l], ln1_b[l], w_q[l], sub_k1[l], sub_k2[l],
                   u_tab[l], v_tab[l], ln2_g[l], ln2_b[l], alpha)
    return x
```

```python
import functools
import math

import jax
import jax.numpy as jnp
from jax import lax
from jax.experimental import pallas as pl
from jax.experimental.pallas import tpu as pltpu

f32 = jnp.float32
bf16 = jnp.bfloat16
i32 = jnp.int32

LANES = 128
SUBLANES = 8
VMEM_LIMIT_BYTES = 56 * 1024 * 1024

GLA_HEADS = 4
GLA_DK = 128
GLA_DV = 256
GLA_GATE_RANK = 16
GLA_TAU = 16.0
GLA_CHUNK = 64
CONV_KERNEL = 31
CONV_HALO = 32
PEER_HEADS = 8
PEER_NKEYS = 128
PEER_TOPK = 16
PEER_PAIRS = PEER_HEADS * PEER_TOPK
PEER_CHUNK = SUBLANES
PEER_NCHUNK = PEER_PAIRS // PEER_CHUNK
LN_EPS = 1e-5


def _cparams(*sem):
    return pltpu.CompilerParams(dimension_semantics=sem, vmem_limit_bytes=VMEM_LIMIT_BYTES)


def _layer_norm_rows(x, g, b=None):
    mu = jnp.mean(x, axis=-1, keepdims=True)
    xc = x - mu
    var = jnp.mean(xc * xc, axis=-1, keepdims=True)
    out = xc * lax.rsqrt(var + LN_EPS) * g
    if b is not None:
        out = out + b
    return out


def _silu(x):
    return x * jax.nn.sigmoid(x)


def _inproj_kernel(x_ref, w_ref, wa_ref, o_ref, a_ref, xb_ref):
    @pl.when(pl.program_id(1) == 0)
    def _():
        xb = x_ref[...].astype(bf16)
        xb_ref[...] = xb
        a_ref[...] = jnp.dot(xb, wa_ref[...], preferred_element_type=f32)

    o_ref[...] = jnp.dot(xb_ref[...], w_ref[...], preferred_element_type=f32)


def _inproj(x2, w_main, w_a, tm=512, tn=1024):
    T, D = x2.shape
    N = w_main.shape[1]
    return pl.pallas_call(
        _inproj_kernel,
        grid=(T // tm, N // tn),
        in_specs=[pl.BlockSpec((tm, D), lambda i, j: (i, 0)),
                  pl.BlockSpec((D, tn), lambda i, j: (0, j)),
                  pl.BlockSpec((D, LANES), lambda i, j: (0, 0))],
        out_specs=[pl.BlockSpec((tm, tn), lambda i, j: (i, j)),
                   pl.BlockSpec((tm, LANES), lambda i, j: (i, 0))],
        out_shape=[jax.ShapeDtypeStruct((T, N), f32), jax.ShapeDtypeStruct((T, LANES), f32)],
        scratch_shapes=[pltpu.VMEM((tm, D), bf16)],
        compiler_params=_cparams("parallel", "arbitrary"),
    )(x2, w_main, w_a)


def _gla_kernel(q_ref, k_ref, v_ref, r_ref, a_ref, wa2_ref, ba_ref, g_ref, o_ref, st_ref):
    C = GLA_CHUNK
    L = q_ref.shape[1]

    @pl.when(pl.program_id(2) == 0)
    def _():
        st_ref[...] = jnp.zeros_like(st_ref)

    row = lax.broadcasted_iota(i32, (C, C), 0)
    col = lax.broadcasted_iota(i32, (C, C), 1)
    causal = row >= col
    tril = causal.astype(f32)
    scale = GLA_DK ** -0.5
    DK, DV = GLA_DK, GLA_DV
    heads = q_ref.shape[2] // DK
    nt = (((1,), (1,)), ((), ()))

    def chunk(n, carry):
        rows = pl.ds(pl.multiple_of(n * C, C), C)
        a_lr = a_ref[0, rows, :]
        for hh in range(heads):
            kc = slice(hh * DK, (hh + 1) * DK)
            vc = slice(hh * DV, (hh + 1) * DV)
            z = jnp.dot(a_lr, wa2_ref[:, kc], preferred_element_type=f32,
                        precision=lax.Precision.HIGHEST) + ba_ref[:, kc]
            log_a = (jnp.minimum(z, 0.0) - jnp.log1p(jnp.exp(-jnp.abs(z)))) / GLA_TAU
            b = jnp.dot(tril, log_a, preferred_element_type=f32, precision=lax.Precision.HIGHEST)
            b_ref = b[C // 2:C // 2 + 1, :]
            b_last = b[C - 1:C, :]
            q = q_ref[0, rows, kc] * scale
            k = k_ref[0, rows, kc]
            v = v_ref[0, rows, vc].astype(bf16)
            q_in = (q * jnp.exp(b - b_ref)).astype(bf16)
            k_in = (k * jnp.exp(b_ref - b)).astype(bf16)
            scores = lax.dot_general(q_in, k_in, nt, preferred_element_type=f32)
            scores = jnp.where(causal, scores, 0.0)
            o = jnp.dot(scores.astype(bf16), v, preferred_element_type=f32)
            st = st_ref[hh]
            q_dec = (q * jnp.exp(b)).astype(bf16)
            o = o + lax.dot_general(q_dec, st.astype(bf16), nt, preferred_element_type=f32)
            k_dec = (k * jnp.exp(b_last - b)).astype(bf16)
            kv_t = lax.dot_general(v, k_dec, (((0,), (0,)), ((), ())), preferred_element_type=f32)
            st_ref[hh] = st * jnp.exp(b_last) + kv_t
            o = _layer_norm_rows(o, g_ref[:, vc])
            o_ref[0, rows, vc] = (o * _silu(r_ref[0, rows, vc])).astype(o_ref.dtype)
        return carry

    lax.fori_loop(0, L // C, chunk, 0)


def _gla(proj3, a3, w_a2p, b_a, gla_g, heads_per_step=4, tl=1024):
    B, L, _ = proj3.shape
    H, hs = GLA_HEADS, heads_per_step
    tl = min(tl, L)
    DK, DV = hs * GLA_DK, hs * GLA_DV
    kq = H // hs
    kv = 2 * kq * DK // DV
    kr = kv + H // hs
    return pl.pallas_call(
        _gla_kernel,
        grid=(B, H // hs, L // tl),
        in_specs=[pl.BlockSpec((1, tl, DK), lambda b, h, l: (b, l, h)),
                  pl.BlockSpec((1, tl, DK), lambda b, h, l: (b, l, kq + h)),
                  pl.BlockSpec((1, tl, DV), lambda b, h, l: (b, l, kv + h)),
                  pl.BlockSpec((1, tl, DV), lambda b, h, l: (b, l, kr + h)),
                  pl.BlockSpec((1, tl, LANES), lambda b, h, l: (b, l, 0)),
                  pl.BlockSpec((LANES, DK), lambda b, h, l: (0, h)),
                  pl.BlockSpec((1, DK), lambda b, h, l: (0, h)),
                  pl.BlockSpec((1, DV), lambda b, h, l: (0, h))],
        out_specs=pl.BlockSpec((1, tl, DV), lambda b, h, l: (b, l, h)),
        out_shape=jax.ShapeDtypeStruct((B, L, H * GLA_DV), bf16),
        scratch_shapes=[pltpu.VMEM((hs, GLA_DV, GLA_DK), f32)],
        compiler_params=_cparams("parallel", "parallel", "arbitrary"),
    )(proj3, proj3, proj3, proj3, a3, w_a2p, b_a, gla_g)


def _conv_kernel(val_ref, gate_ref, w_ref, cb_ref, g_ref, b_ref, o_ref, glu_ref, acc_ref):
    tl = val_ref.shape[1]
    W = val_ref.shape[2]
    rc = 64

    @pl.when(pl.program_id(1) == 0)
    def _():
        glu_ref[0:CONV_HALO, :] = jnp.zeros((CONV_HALO, W), f32)

    @pl.when(pl.program_id(1) != 0)
    def _():
        glu_ref[0:CONV_HALO, :] = glu_ref[tl:tl + CONV_HALO, :]

    glu_ref[CONV_HALO:CONV_HALO + tl, :] = val_ref[0] * jax.nn.sigmoid(gate_ref[0])
    lead = CONV_HALO - (CONV_KERNEL - 1)

    def strip(s, carry):
        cols = pl.ds(pl.multiple_of(s * LANES, LANES), LANES)
        wj = w_ref[:, cols]
        bias = cb_ref[:, cols]
        for r0 in range(0, tl, rc):
            acc = jnp.zeros((rc, LANES), f32) + bias
            for j in range(CONV_KERNEL):
                acc = acc + glu_ref[pl.ds(r0 + lead + j, rc), cols] * wj[j:j + 1, :]
            acc_ref[pl.ds(r0, rc), cols] = acc
        return carry

    lax.fori_loop(0, W // LANES, strip, 0)
    y = _layer_norm_rows(acc_ref[...], g_ref[...], b_ref[...])
    o_ref[0] = _silu(y).astype(o_ref.dtype)


def _conv(proj3, conv_wp, conv_b, ln_g, ln_b, val_blk, tl=256):
    B, L, _ = proj3.shape
    W = conv_wp.shape[1]
    return pl.pallas_call(
        _conv_kernel,
        grid=(B, L // tl),
        in_specs=[pl.BlockSpec((1, tl, W), lambda b, l: (b, l, val_blk)),
                  pl.BlockSpec((1, tl, W), lambda b, l: (b, l, val_blk + 1)),
                  pl.BlockSpec(conv_wp.shape, lambda b, l: (0, 0)),
                  pl.BlockSpec((1, W), lambda b, l: (0, 0)),
                  pl.BlockSpec((1, W), lambda b, l: (0, 0)),
                  pl.BlockSpec((1, W), lambda b, l: (0, 0))],
        out_specs=pl.BlockSpec((1, tl, W), lambda b, l: (b, l, 0)),
        out_shape=jax.ShapeDtypeStruct((B, L, W), bf16),
        scratch_shapes=[pltpu.VMEM((tl + CONV_HALO, W), f32), pltpu.VMEM((tl, W), f32)],
        compiler_params=_cparams("parallel", "arbitrary"),
    )(proj3, proj3, conv_wp, conv_b, ln_g, ln_b)


def _outproj_kernel(alpha, ga_ref, cv_ref, x_ref, w1_ref, w2_ref, g_ref, b_ref, h_ref):
    mix = jnp.dot(ga_ref[...], w1_ref[...], preferred_element_type=f32)
    mix = mix + jnp.dot(cv_ref[...], w2_ref[...], preferred_element_type=f32)
    h_ref[...] = _layer_norm_rows(alpha * x_ref[...] + mix, g_ref[...], b_ref[...])


def _outproj(gla_out, conv_out, x2, w_o1, w_o2, g, b, alpha, tm=256):
    T, D = x2.shape
    W1 = gla_out.shape[1]
    W2 = conv_out.shape[1]
    return pl.pallas_call(
        functools.partial(_outproj_kernel, alpha),
        grid=(T // tm,),
        in_specs=[pl.BlockSpec((tm, W1), lambda i: (i, 0)),
                  pl.BlockSpec((tm, W2), lambda i: (i, 0)),
                  pl.BlockSpec((tm, D), lambda i: (i, 0)),
                  pl.BlockSpec((W1, D), lambda i: (0, 0)),
                  pl.BlockSpec((W2, D), lambda i: (0, 0)),
                  pl.BlockSpec((1, D), lambda i: (0, 0)),
                  pl.BlockSpec((1, D), lambda i: (0, 0))],
        out_specs=pl.BlockSpec((tm, D), lambda i: (i, 0)),
        out_shape=jax.ShapeDtypeStruct((T, D), f32),
        compiler_params=_cparams("parallel"),
    )(gla_out, conv_out, x2, w_o1, w_o2, g, b)


def _qproj_kernel(h_ref, w_ref, q_ref):
    q_ref[...] = jnp.dot(h_ref[...].astype(bf16), w_ref[...],
                         preferred_element_type=f32).astype(q_ref.dtype)


def _qproj(h2, w_q, tm=256):
    T, D = h2.shape
    N = w_q.shape[1]
    return pl.pallas_call(
        _qproj_kernel,
        grid=(T // tm,),
        in_specs=[pl.BlockSpec((tm, D), lambda i: (i, 0)),
                  pl.BlockSpec((D, N), lambda i: (0, 0))],
        out_specs=pl.BlockSpec((tm, N), lambda i: (i, 0)),
        out_shape=jax.ShapeDtypeStruct((T, N), bf16),
        compiler_params=_cparams("parallel"),
    )(h2, w_q)


def _topk_rows(s, k):
    R = s.shape[0]
    iota = lax.broadcasted_iota(i32, s.shape, 0)
    vals, idxs = [], []
    for _ in range(k):
        m = jnp.max(s, axis=0, keepdims=True)
        idx = jnp.min(jnp.where(s == m, iota, R), axis=0, keepdims=True)
        vals.append(m)
        idxs.append(idx)
        s = jnp.where(iota == idx, -jnp.inf, s)
    return vals, idxs


def _staircase_candidates(s1t, i1, s2t, i2):
    K = PEER_TOPK
    s1c, i1c = jnp.concatenate(s1t, axis=0), jnp.concatenate(i1, axis=0)
    s2c, i2c = jnp.concatenate(s2t, axis=0), jnp.concatenate(i2, axis=0)
    sub = lax.broadcasted_iota(i32, (SUBLANES, s1c.shape[1]), 0)
    vals, idxs = [], []
    a = 0
    while K // (a + 1) > 1:
        n_b = K // (a + 1)
        for b0 in range(0, n_b, SUBLANES):
            v = s1t[a] + s2c[b0:b0 + SUBLANES]
            if n_b - b0 < SUBLANES:
                v = jnp.where(sub < n_b - b0, v, -jnp.inf)
            vals.append(v)
            idxs.append(i1[a] * PEER_NKEYS + i2c[b0:b0 + SUBLANES])
        a += 1
    assert K - a == SUBLANES
    vals.append(s1c[a:K] + s2t[0])
    idxs.append(i1c[a:K] * PEER_NKEYS + i2[0])
    return jnp.concatenate(vals, axis=0), jnp.concatenate(idxs, axis=0)


def _bitonic_sort_groups(keys, vals):
    ng = len(keys)
    n = ng * SUBLANES
    sub = lax.broadcasted_iota(i32, keys[0].shape, 0)
    k = 2
    while k <= n:
        j = k // 2
        while j >= 1:
            if j >= SUBLANES:
                gj = j // SUBLANES
                for g in range(ng):
                    if g & gj:
                        continue
                    p = g | gj
                    asc = ((g * SUBLANES) & k) == 0
                    swap = keys[g] > keys[p] if asc else keys[g] < keys[p]
                    kg = jnp.where(swap, keys[p], keys[g])
                    kp = jnp.where(swap, keys[g], keys[p])
                    vg = jnp.where(swap, vals[p], vals[g])
                    vp = jnp.where(swap, vals[g], vals[p])
                    keys[g], keys[p], vals[g], vals[p] = kg, kp, vg, vp
            else:
                low = (sub & j) == 0
                for g in range(ng):
                    x, y = keys[g], vals[g]
                    px = jnp.where(low, pltpu.roll(x, SUBLANES - j, 0), pltpu.roll(x, j, 0))
                    py = jnp.where(low, pltpu.roll(y, SUBLANES - j, 0), pltpu.roll(y, j, 0))
                    if k >= SUBLANES:
                        asc = ((g * SUBLANES) & k) == 0
                        take_min = low if asc else jnp.logical_not(low)
                    else:
                        take_min = low == ((sub & k) == 0)
                    sel = (take_min & (px < x)) | (jnp.logical_not(take_min) & (px > x))
                    keys[g] = jnp.where(sel, px, x)
                    vals[g] = jnp.where(sel, py, y)
            j //= 2
        k *= 2
    return keys, vals


def _route_kernel(nba, nbv, q_ref, k1_ref, k2_ref, e_ref, eloca_ref, elocv_ref, gate_ref, st_ref):
    Tt = q_ref.shape[0]
    half = q_ref.shape[1] // PEER_HEADS // 2
    nt = (((1,), (1,)), ((), ()))
    e_rows, g_rows = [], []
    for h in range(PEER_HEADS):
        q1 = q_ref[:, (2 * h) * half:(2 * h + 1) * half]
        q2 = q_ref[:, (2 * h + 1) * half:(2 * h + 2) * half]
        s1 = lax.dot_general(k1_ref[h], q1, nt, preferred_element_type=f32)
        s2 = lax.dot_general(k2_ref[h], q2, nt, preferred_element_type=f32)
        s1t, i1 = _topk_rows(s1, PEER_TOPK)
        s2t, i2 = _topk_rows(s2, PEER_TOPK)
        cand, cidx = _staircase_candidates(s1t, i1, s2t, i2)
        iota = lax.broadcasted_iota(i32, cand.shape, 0)
        sc, ex = [], []
        for _ in range(PEER_TOPK):
            m = jnp.max(cand, axis=0, keepdims=True)
            pos = jnp.min(jnp.where(cand == m, iota, cand.shape[0]), axis=0, keepdims=True)
            hit = iota == pos
            ex.append(jnp.sum(jnp.where(hit, cidx, 0), axis=0, keepdims=True))
            sc.append(m)
            cand = jnp.where(hit, -jnp.inf, cand)
        p = [jnp.exp(s - sc[0]) for s in sc]
        denom = p[0]
        for t in p[1:]:
            denom = denom + t
        e_rows += ex
        g_rows += [t / denom for t in p]
    ng = PEER_PAIRS // SUBLANES
    keys = [jnp.concatenate(e_rows[g * SUBLANES:(g + 1) * SUBLANES], axis=0) for g in range(ng)]
    vals = [jnp.concatenate(g_rows[g * SUBLANES:(g + 1) * SUBLANES], axis=0) for g in range(ng)]
    keys, vals = _bitonic_sort_groups(keys, vals)
    e_tok = jnp.concatenate(keys, axis=0).T
    e_ref[...] = e_tok
    gate_ref[...] = jnp.concatenate(vals, axis=0).T
    lane = lax.broadcasted_iota(i32, (Tt, LANES), 1)
    starts = jnp.zeros((Tt, LANES), i32)
    for nb, eloc_ref, off in ((nba, eloca_ref, 0), (nbv, elocv_ref, PEER_ST_STRIDE)):
        bsz = (PEER_NKEYS * PEER_NKEYS) // nb
        for b in range(nb):
            eloc_ref[b] = jnp.clip(e_tok - b * bsz, 0, bsz - 1)
        for b in range(1, nb + 1):
            cnt = jnp.sum((e_tok < b * bsz).astype(i32), axis=1, keepdims=True)
            starts = jnp.where(lane == off + b, cnt, starts)
    st_ref[...] = starts


def _route(q, k1, k2, nba, nbv, tt=128):
    T, QW = q.shape
    H, NK, half = k1.shape
    return pl.pallas_call(
        functools.partial(_route_kernel, nba, nbv),
        grid=(T // tt,),
        in_specs=[pl.BlockSpec((tt, QW), lambda i: (i, 0)),
                  pl.BlockSpec((H, NK, half), lambda i: (0, 0, 0)),
                  pl.BlockSpec((H, NK, half), lambda i: (0, 0, 0))],
        out_specs=[pl.BlockSpec((tt, PEER_PAIRS), lambda i: (i, 0)),
                   pl.BlockSpec((nba, tt, PEER_PAIRS), lambda i: (0, i, 0)),
                   pl.BlockSpec((nbv, tt, PEER_PAIRS), lambda i: (0, i, 0)),
                   pl.BlockSpec((tt, PEER_PAIRS), lambda i: (i, 0)),
                   pl.BlockSpec((tt, LANES), lambda i: (i, 0))],
        out_shape=[jax.ShapeDtypeStruct((T, PEER_PAIRS), i32),
                   jax.ShapeDtypeStruct((nba, T, PEER_PAIRS), i32),
                   jax.ShapeDtypeStruct((nbv, T, PEER_PAIRS), i32),
                   jax.ShapeDtypeStruct((T, PEER_PAIRS), f32),
                   jax.ShapeDtypeStruct((T, LANES), i32)],
        compiler_params=_cparams("parallel"),
    )(q, k1, k2)


PEER_ST_STRIDE = SUBLANES


def _chunk_range(st_smem, idx, chunk):
    s0 = st_smem[idx]
    s1 = st_smem[idx + 1]
    sh = int(math.log2(chunk))
    cs = lax.shift_right_logical(s0, sh)
    ce = jnp.where(s1 > s0, lax.shift_right_logical(s1 + (chunk - 1), sh), cs)
    return s0, s1, cs, ce


def _prefetch_lists(step, nsteps, copies_for):
    slot = step & 1

    @pl.when(step == 0)
    def _():
        for c in copies_for(step, slot):
            c.start()

    @pl.when(step + 1 < nsteps)
    def _():
        for c in copies_for(step + 1, 1 - slot):
            c.start()

    for c in copies_for(step, slot):
        c.wait()
    return slot


def _gelu_exact(x):
    return 0.5 * x * (1.0 + lax.erf(x * (2.0 ** -0.5)))


def _fold(a, b, s, sub):
    ta = a + pltpu.roll(a, s, 0)
    tb = b + pltpu.roll(b, SUBLANES - s, 0)
    return jnp.where((sub & s) != 0, ta, tb)


def _fold_first(vs, sub):
    upper = (sub & 4) != 0
    return tuple(jnp.where(upper, vs[r + 4], vs[r]) + pltpu.roll(jnp.where(upper, vs[r], vs[r + 4]), 4, 0)
                 for r in range(4))


def _fold_rest(ms, sub):
    return _fold(_fold(ms[3], ms[1], 2, sub), _fold(ms[2], ms[0], 2, sub), 1, sub)


def _act_kernel(nba, nbv, eloc_hbm, st_hbm, h_ref, u_ref, e_ref, gate_ref, w_ref,
                e_smem, st_smem, stage_ref, act_ref, sem):
    i = pl.program_id(0)
    b = pl.program_id(1)
    Tk = h_ref.shape[0]
    T = Tk * pl.num_programs(0)
    n = Tk * PEER_PAIRS
    ns = Tk * PEER_ST_STRIDE
    lg = int(math.log2(nba))

    def copies_for(step, slot):
        si = lax.shift_right_logical(step, lg)
        sb = step & (nba - 1)
        src_e = eloc_hbm.at[pl.ds(pl.multiple_of((sb * T + si * Tk) * PEER_PAIRS, n), n)]
        src_s = st_hbm.at[pl.ds(pl.multiple_of(si * ns, ns), ns)]
        return (pltpu.make_async_copy(src_e, e_smem.at[pl.ds(pl.multiple_of(slot * n, n), n)], sem.at[slot, 0]),
                pltpu.make_async_copy(src_s, st_smem.at[pl.ds(pl.multiple_of(slot * ns, ns), ns)], sem.at[slot, 1]))

    slot = _prefetch_lists(i * nba + b, pl.num_programs(0) * nba, copies_for)
    eoff = slot * n
    soff = slot * ns + b
    sub = lax.broadcasted_iota(i32, (SUBLANES, LANES), 0)

    def token(t, carry):
        s0, s1, cs, ce = _chunk_range(st_smem, soff + t * PEER_ST_STRIDE, PEER_CHUNK)
        hrow = h_ref[t]
        base = eoff + t * PEER_PAIRS

        def flush(c, folded):
            first = pl.multiple_of(c * PEER_CHUNK, PEER_CHUNK)
            part = _fold_rest(folded, sub)
            pos = sub + first
            pltpu.store(stage_ref.at[t, pl.ds(first, PEER_CHUNK), :], part,
                        mask=(pos >= s0) & (pos < s1))

        def expert_ids(c):
            first = base + jnp.minimum(c, PEER_NCHUNK - 1) * PEER_CHUNK
            return tuple(e_smem[first + k] for k in range(PEER_CHUNK))

        def chunk(c, pending):
            c_prev, folded_prev, ids = pending
            halves = []
            for k in range(PEER_CHUNK):
                pr = hrow * u_ref[ids[k]].astype(f32)
                halves.append(pr[:SUBLANES] + pr[SUBLANES:])
            flush(c_prev, folded_prev)
            return c, _fold_first(halves, sub), expert_ids(c + 1)

        zero = jnp.zeros((SUBLANES, LANES), f32)
        pending = (jnp.minimum(cs, PEER_NCHUNK - 1), (zero,) * (PEER_CHUNK // 2), expert_ids(cs))
        c_last, folded_last, _ = lax.fori_loop(cs, ce, chunk, pending)
        flush(c_last, folded_last)
        return carry

    lax.fori_loop(0, Tk, token, 0)

    @pl.when(b == nba - 1)
    def _():
        unroll = 4

        def reduce_tokens(g, carry):
            for k in range(unroll):
                t = g * unroll + k
                act_ref[pl.ds(t, 1), :] = jnp.sum(stage_ref[t].T, axis=0, keepdims=True)
            return carry

        lax.fori_loop(0, Tk // unroll, reduce_tokens, 0)
        w = gate_ref[...] * _gelu_exact(act_ref[...])
        blk = lax.shift_right_logical(e_ref[...], int(math.log2(PEER_NKEYS * PEER_NKEYS // nbv)))
        for bb in range(nbv):
            w_ref[bb] = jnp.where(blk == bb, w, 0.0)


def _peer_act(eloc, starts, h3, u3, e_tok, gate, nba, nbv, tk):
    T = h3.shape[0]
    E = u3.shape[0]
    bsz = E // nba
    rowblk = h3.shape[1:]
    return pl.pallas_call(
        functools.partial(_act_kernel, nba, nbv),
        grid=(T // tk, nba),
        in_specs=[pl.BlockSpec(memory_space=pl.ANY),
                  pl.BlockSpec(memory_space=pl.ANY),
                  pl.BlockSpec((tk,) + rowblk, lambda i, b: (i, 0, 0)),
                  pl.BlockSpec((bsz,) + rowblk, lambda i, b: (b, 0, 0)),
                  pl.BlockSpec((tk, PEER_PAIRS), lambda i, b: (i, 0)),
                  pl.BlockSpec((tk, PEER_PAIRS), lambda i, b: (i, 0))],
        out_specs=pl.BlockSpec((nbv, tk, PEER_PAIRS), lambda i, b: (0, i, 0)),
        out_shape=jax.ShapeDtypeStruct((nbv, T, PEER_PAIRS), f32),
        scratch_shapes=[pltpu.SMEM((2 * tk * PEER_PAIRS,), i32),
                        pltpu.SMEM((2 * tk * PEER_ST_STRIDE,), i32),
                        pltpu.VMEM((tk, PEER_PAIRS, LANES), f32),
                        pltpu.VMEM((tk, PEER_PAIRS), f32),
                        pltpu.SemaphoreType.DMA((2, 2))],
        compiler_params=_cparams("arbitrary", "arbitrary"),
    )(eloc, starts, h3, u3, e_tok, gate)


PEER_VAL_CHUNK = 4 * PEER_CHUNK
PEER_VAL_NACC = 4


def _val_kernel(eloc_hbm, st_hbm, w_hbm, v_hbm, o_ref, e_smem, st_smem, w_smem, v_ref, sem, vsem):
    b = pl.program_id(0)
    i = pl.program_id(1)
    ni = pl.num_programs(1)
    Tk = o_ref.shape[1]
    T = Tk * ni
    n = Tk * PEER_PAIRS
    ns = Tk * PEER_ST_STRIDE
    bsz = v_ref.shape[0]

    @pl.when(i == 0)
    def _():
        cp = pltpu.make_async_copy(v_hbm.at[pl.ds(b * bsz, bsz)], v_ref, vsem)
        cp.start()
        cp.wait()

    def copies_for(step, slot):
        sb = step // ni
        si = step - sb * ni
        lists = pl.ds(pl.multiple_of((sb * T + si * Tk) * PEER_PAIRS, n), n)
        dst = pl.ds(pl.multiple_of(slot * n, n), n)
        return (pltpu.make_async_copy(eloc_hbm.at[lists], e_smem.at[dst], sem.at[slot, 0]),
                pltpu.make_async_copy(st_hbm.at[pl.ds(pl.multiple_of(si * ns, ns), ns)],
                                      st_smem.at[pl.ds(pl.multiple_of(slot * ns, ns), ns)], sem.at[slot, 1]),
                pltpu.make_async_copy(w_hbm.at[lists], w_smem.at[dst], sem.at[slot, 2]))

    slot = _prefetch_lists(b * ni + i, pl.num_programs(0) * ni, copies_for)
    eoff = slot * n
    soff = slot * ns + b

    def token(t, carry):
        _, _, cs, ce = _chunk_range(st_smem, soff + t * PEER_ST_STRIDE, PEER_VAL_CHUNK)
        base = eoff + t * PEER_PAIRS

        def chunk(c, accs):
            first = base + c * PEER_VAL_CHUNK
            accs = list(accs)
            for k in range(PEER_VAL_CHUNK):
                term = w_smem[first + k] * v_ref[e_smem[first + k]].astype(f32)
                accs[k % PEER_VAL_NACC] = accs[k % PEER_VAL_NACC] + term
            return tuple(accs)

        zero = jnp.zeros(o_ref.shape[2:], f32)
        accs = lax.fori_loop(cs, ce, chunk, (zero,) * PEER_VAL_NACC)
        o_ref[0, t] = (accs[0] + accs[1]) + (accs[2] + accs[3])
        return carry

    lax.fori_loop(0, Tk, token, 0)


def _peer_val(eloc, starts, w, v3, nb, tk):
    T = eloc.shape[0] // (nb * PEER_PAIRS)
    E = v3.shape[0]
    bsz = E // nb
    rowblk = v3.shape[1:]
    return pl.pallas_call(
        _val_kernel,
        grid=(nb, T // tk),
        in_specs=[pl.BlockSpec(memory_space=pl.ANY),
                  pl.BlockSpec(memory_space=pl.ANY),
                  pl.BlockSpec(memory_space=pl.ANY),
                  pl.BlockSpec(memory_space=pl.ANY)],
        out_specs=pl.BlockSpec((1, tk) + rowblk, lambda b, i: (b, i, 0, 0)),
        out_shape=jax.ShapeDtypeStruct((nb, T) + rowblk, f32),
        scratch_shapes=[pltpu.SMEM((2 * tk * PEER_PAIRS,), i32),
                        pltpu.SMEM((2 * tk * PEER_ST_STRIDE,), i32),
                        pltpu.SMEM((2 * tk * PEER_PAIRS,), f32),
                        pltpu.VMEM((bsz,) + rowblk, v3.dtype),
                        pltpu.SemaphoreType.DMA((2, 3)),
                        pltpu.SemaphoreType.DMA(())],
        compiler_params=_cparams("arbitrary", "arbitrary"),
    )(eloc, starts, w, v3)


def _ln2_kernel(alpha, h_ref, f_ref, g_ref, b_ref, o_ref):
    o_ref[...] = _layer_norm_rows(alpha * h_ref[...] + jnp.sum(f_ref[...], axis=0), g_ref[...], b_ref[...])


def _ln2(h2, ffn, g, b, alpha, tm=512):
    T, D = h2.shape
    return pl.pallas_call(
        functools.partial(_ln2_kernel, alpha),
        grid=(T // tm,),
        in_specs=[pl.BlockSpec((tm, D), lambda i: (i, 0)),
                  pl.BlockSpec((ffn.shape[0], tm, D), lambda i: (0, i, 0)),
                  pl.BlockSpec((1, D), lambda i: (0, 0)),
                  pl.BlockSpec((1, D), lambda i: (0, 0))],
        out_specs=pl.BlockSpec((tm, D), lambda i: (i, 0)),
        out_shape=jax.ShapeDtypeStruct((T, D), f32),
        compiler_params=_cparams("parallel"),
    )(h2, ffn, g, b)


PEER_ACT_NBLOCKS = 4
PEER_VAL_NBLOCKS = 2
PEER_ACT_TOKENS = 128
PEER_VAL_TOKENS = 256


def _layer(x, w_in, w_a2, b_a, gla_norm_g, conv_w, conv_b, conv_ln_g, conv_ln_b, w_out,
           ln1_g, ln1_b, w_q, sub_k1, sub_k2, u_tab, v_tab, ln2_g, ln2_b, alpha):
    B, L, D = x.shape
    T = B * L
    kw = GLA_HEADS * GLA_DK
    vw = GLA_HEADS * GLA_DV
    split_r = 2 * kw + 2 * vw
    split_a = split_r + GLA_GATE_RANK
    cw = conv_w.shape[1]
    x2 = x.reshape(T, D)
    w_main = jnp.concatenate([w_in[:, :split_r], w_in[:, split_a:]], axis=1).astype(bf16)
    w_a = jnp.pad(w_in[:, split_r:split_a], ((0, 0), (0, LANES - GLA_GATE_RANK))).astype(bf16)
    w_a2p = jnp.pad(w_a2, ((0, LANES - GLA_GATE_RANK), (0, 0)))
    conv_wp = jnp.pad(conv_w, ((0, CONV_HALO - CONV_KERNEL), (0, 0)))
    proj, a_lr = _inproj(x2, w_main, w_a)
    proj3 = proj.reshape(B, L, -1)
    gla_out = _gla(proj3, a_lr.reshape(B, L, LANES), w_a2p, b_a.reshape(1, kw),
                   gla_norm_g.reshape(1, vw))
    conv_out = _conv(proj3, conv_wp, conv_b.reshape(1, cw), conv_ln_g.reshape(1, cw),
                     conv_ln_b.reshape(1, cw), val_blk=split_r // cw)
    h2 = _outproj(gla_out.reshape(T, vw), conv_out.reshape(T, cw), x2,
                  w_out[:vw].astype(bf16), w_out[vw:].astype(bf16),
                  ln1_g.reshape(1, D), ln1_b.reshape(1, D), alpha)
    q = _qproj(h2, w_q.astype(bf16))
    nba, nbv = PEER_ACT_NBLOCKS, PEER_VAL_NBLOCKS
    e_tok, eloc_a, eloc_v, gate, starts = _route(q, sub_k1.astype(bf16), sub_k2.astype(bf16), nba, nbv)
    rows = D // LANES
    u3 = u_tab.astype(bf16).reshape(-1, rows, LANES)
    v3 = v_tab.astype(bf16).reshape(-1, rows, LANES)
    h3 = h2.reshape(T, rows, LANES)
    starts_a = starts[:, :PEER_ST_STRIDE].reshape(-1)
    starts_v = starts[:, PEER_ST_STRIDE:2 * PEER_ST_STRIDE].reshape(-1)
    w = _peer_act(eloc_a.reshape(-1), starts_a, h3, u3, e_tok, gate, nba, nbv, PEER_ACT_TOKENS)
    ffn = _peer_val(eloc_v.reshape(-1), starts_v, w.reshape(-1), v3, nbv, PEER_VAL_TOKENS)
    return _ln2(h2, ffn.reshape(nbv, T, D), ln2_g.reshape(1, D), ln2_b.reshape(1, D), alpha).reshape(B, L, D)


def kernel(x, w_in, w_a2, b_a, gla_norm_g, conv_w, conv_b, conv_ln_g, conv_ln_b, w_out, ln1_g, ln1_b, w_q, sub_k1, sub_k2, u_tab, v_tab, ln2_g, ln2_b):
    depth = w_in.shape[0]
    alpha = (2.0 * depth) ** 0.25
    for l in range(depth):
        x = _layer(x, w_in[l], w_a2[l], b_a[l], gla_norm_g[l], conv_w[l], conv_b[l], conv_ln_g[l],
                   conv_ln_b[l], w_out[l], ln1_g[l], ln1_b[l], w_q[l], sub_k1[l], sub_k2[l],
                   u_tab[l], v_tab[l], ln2_g[l], ln2_b[l], alpha)
    return x
```

```python
import functools
import math

import jax
import jax.numpy as jnp
from jax import lax
from jax.experimental import pallas as pl
from jax.experimental.pallas import tpu as pltpu

f32 = jnp.float32
bf16 = jnp.bfloat16
i32 = jnp.int32

LANES = 128
SUBLANES = 8
VMEM_LIMIT_BYTES = 60 * 1024 * 1024

GLA_HEADS = 4
GLA_DK = 128
GLA_DV = 256
GLA_GATE_RANK = 16
GLA_TAU = 16.0
GLA_CHUNK = 64
CONV_KERNEL = 31
CONV_HALO = 32
PEER_HEADS = 8
PEER_NKEYS = 128
PEER_TOPK = 16
PEER_PAIRS = PEER_HEADS * PEER_TOPK
PEER_CHUNK = SUBLANES
PEER_NCHUNK = PEER_PAIRS // PEER_CHUNK
LN_EPS = 1e-5


def _cparams(*sem):
    return pltpu.CompilerParams(dimension_semantics=sem, vmem_limit_bytes=VMEM_LIMIT_BYTES)


def _layer_norm_rows(x, g, b=None):
    mu = jnp.mean(x, axis=-1, keepdims=True)
    xc = x - mu
    var = jnp.mean(xc * xc, axis=-1, keepdims=True)
    out = xc * lax.rsqrt(var + LN_EPS) * g
    if b is not None:
        out = out + b
    return out


def _silu(x):
    return x * jax.nn.sigmoid(x)


def _inproj_kernel(x_ref, w_ref, wa_ref, o_ref, a_ref, xb_ref):
    @pl.when(pl.program_id(1) == 0)
    def _():
        xb = x_ref[...].astype(bf16)
        xb_ref[...] = xb
        a_ref[...] = jnp.dot(xb, wa_ref[...], preferred_element_type=f32)

    o_ref[...] = jnp.dot(xb_ref[...], w_ref[...], preferred_element_type=f32)


def _inproj(x2, w_main, w_a, tm=512, tn=1024):
    T, D = x2.shape
    N = w_main.shape[1]
    return pl.pallas_call(
        _inproj_kernel,
        grid=(T // tm, N // tn),
        in_specs=[pl.BlockSpec((tm, D), lambda i, j: (i, 0)),
                  pl.BlockSpec((D, tn), lambda i, j: (0, j)),
                  pl.BlockSpec((D, LANES), lambda i, j: (0, 0))],
        out_specs=[pl.BlockSpec((tm, tn), lambda i, j: (i, j)),
                   pl.BlockSpec((tm, LANES), lambda i, j: (i, 0))],
        out_shape=[jax.ShapeDtypeStruct((T, N), f32), jax.ShapeDtypeStruct((T, LANES), f32)],
        scratch_shapes=[pltpu.VMEM((tm, D), bf16)],
        compiler_params=_cparams("parallel", "arbitrary"),
    )(x2, w_main, w_a)


def _gla_kernel(q_ref, k_ref, v_ref, r_ref, a_ref, wa2_ref, ba_ref, g_ref, o_ref, st_ref):
    C = GLA_CHUNK
    L = q_ref.shape[1]

    @pl.when(pl.program_id(2) == 0)
    def _():
        st_ref[...] = jnp.zeros_like(st_ref)

    row = lax.broadcasted_iota(i32, (C, C), 0)
    col = lax.broadcasted_iota(i32, (C, C), 1)
    causal = row >= col
    tril = causal.astype(f32)
    scale = GLA_DK ** -0.5
    DK, DV = GLA_DK, GLA_DV
    heads = q_ref.shape[2] // DK
    nt = (((1,), (1,)), ((), ()))

    def chunk(n, carry):
        rows = pl.ds(pl.multiple_of(n * C, C), C)
        a_lr = a_ref[0, rows, :]
        for hh in range(heads):
            kc = slice(hh * DK, (hh + 1) * DK)
            vc = slice(hh * DV, (hh + 1) * DV)
            z = jnp.dot(a_lr, wa2_ref[:, kc], preferred_element_type=f32,
                        precision=lax.Precision.HIGHEST) + ba_ref[:, kc]
            log_a = (jnp.minimum(z, 0.0) - jnp.log1p(jnp.exp(-jnp.abs(z)))) / GLA_TAU
            b = jnp.dot(tril, log_a, preferred_element_type=f32, precision=lax.Precision.HIGHEST)
            b_ref = b[C // 2:C // 2 + 1, :]
            b_last = b[C - 1:C, :]
            q = q_ref[0, rows, kc] * scale
            k = k_ref[0, rows, kc]
            v = v_ref[0, rows, vc].astype(bf16)
            q_in = (q * jnp.exp(b - b_ref)).astype(bf16)
            k_in = (k * jnp.exp(b_ref - b)).astype(bf16)
            scores = lax.dot_general(q_in, k_in, nt, preferred_element_type=f32)
            scores = jnp.where(causal, scores, 0.0)
            o = jnp.dot(scores.astype(bf16), v, preferred_element_type=f32)
            st = st_ref[hh]
            q_dec = (q * jnp.exp(b)).astype(bf16)
            o = o + lax.dot_general(q_dec, st.astype(bf16), nt, preferred_element_type=f32)
            k_dec = (k * jnp.exp(b_last - b)).astype(bf16)
            kv_t = lax.dot_general(v, k_dec, (((0,), (0,)), ((), ())), preferred_element_type=f32)
            st_ref[hh] = st * jnp.exp(b_last) + kv_t
            o = _layer_norm_rows(o, g_ref[:, vc])
            o_ref[0, rows, vc] = (o * _silu(r_ref[0, rows, vc])).astype(o_ref.dtype)
        return carry

    lax.fori_loop(0, L // C, chunk, 0)


def _gla(proj3, a3, w_a2p, b_a, gla_g, heads_per_step=4, tl=1024):
    B, L, _ = proj3.shape
    H, hs = GLA_HEADS, heads_per_step
    tl = min(tl, L)
    DK, DV = hs * GLA_DK, hs * GLA_DV
    kq = H // hs
    kv = 2 * kq * DK // DV
    kr = kv + H // hs
    return pl.pallas_call(
        _gla_kernel,
        grid=(B, H // hs, L // tl),
        in_specs=[pl.BlockSpec((1, tl, DK), lambda b, h, l: (b, l, h)),
                  pl.BlockSpec((1, tl, DK), lambda b, h, l: (b, l, kq + h)),
                  pl.BlockSpec((1, tl, DV), lambda b, h, l: (b, l, kv + h)),
                  pl.BlockSpec((1, tl, DV), lambda b, h, l: (b, l, kr + h)),
                  pl.BlockSpec((1, tl, LANES), lambda b, h, l: (b, l, 0)),
                  pl.BlockSpec((LANES, DK), lambda b, h, l: (0, h)),
                  pl.BlockSpec((1, DK), lambda b, h, l: (0, h)),
                  pl.BlockSpec((1, DV), lambda b, h, l: (0, h))],
        out_specs=pl.BlockSpec((1, tl, DV), lambda b, h, l: (b, l, h)),
        out_shape=jax.ShapeDtypeStruct((B, L, H * GLA_DV), bf16),
        scratch_shapes=[pltpu.VMEM((hs, GLA_DV, GLA_DK), f32)],
        compiler_params=_cparams("parallel", "parallel", "arbitrary"),
    )(proj3, proj3, proj3, proj3, a3, w_a2p, b_a, gla_g)


def _conv_kernel(val_ref, gate_ref, w_ref, cb_ref, g_ref, b_ref, o_ref, glu_ref, acc_ref):
    tl = val_ref.shape[1]
    W = val_ref.shape[2]
    rc = 64

    @pl.when(pl.program_id(1) == 0)
    def _():
        glu_ref[0:CONV_HALO, :] = jnp.zeros((CONV_HALO, W), f32)

    @pl.when(pl.program_id(1) != 0)
    def _():
        glu_ref[0:CONV_HALO, :] = glu_ref[tl:tl + CONV_HALO, :]

    glu_ref[CONV_HALO:CONV_HALO + tl, :] = val_ref[0] * jax.nn.sigmoid(gate_ref[0])
    lead = CONV_HALO - (CONV_KERNEL - 1)

    def strip(s, carry):
        cols = pl.ds(pl.multiple_of(s * LANES, LANES), LANES)
        wj = w_ref[:, cols]
        bias = cb_ref[:, cols]
        for r0 in range(0, tl, rc):
            acc = jnp.zeros((rc, LANES), f32) + bias
            for j in range(CONV_KERNEL):
                acc = acc + glu_ref[pl.ds(r0 + lead + j, rc), cols] * wj[j:j + 1, :]
            acc_ref[pl.ds(r0, rc), cols] = acc
        return carry

    lax.fori_loop(0, W // LANES, strip, 0)
    y = _layer_norm_rows(acc_ref[...], g_ref[...], b_ref[...])
    o_ref[0] = _silu(y).astype(o_ref.dtype)


def _conv(proj3, conv_wp, conv_b, ln_g, ln_b, val_blk, tl=256):
    B, L, _ = proj3.shape
    W = conv_wp.shape[1]
    return pl.pallas_call(
        _conv_kernel,
        grid=(B, L // tl),
        in_specs=[pl.BlockSpec((1, tl, W), lambda b, l: (b, l, val_blk)),
                  pl.BlockSpec((1, tl, W), lambda b, l: (b, l, val_blk + 1)),
                  pl.BlockSpec(conv_wp.shape, lambda b, l: (0, 0)),
                  pl.BlockSpec((1, W), lambda b, l: (0, 0)),
                  pl.BlockSpec((1, W), lambda b, l: (0, 0)),
                  pl.BlockSpec((1, W), lambda b, l: (0, 0))],
        out_specs=pl.BlockSpec((1, tl, W), lambda b, l: (b, l, 0)),
        out_shape=jax.ShapeDtypeStruct((B, L, W), bf16),
        scratch_shapes=[pltpu.VMEM((tl + CONV_HALO, W), f32), pltpu.VMEM((tl, W), f32)],
        compiler_params=_cparams("parallel", "arbitrary"),
    )(proj3, proj3, conv_wp, conv_b, ln_g, ln_b)


def _outproj_kernel(alpha, ga_ref, cv_ref, x_ref, w1_ref, w2_ref, g_ref, b_ref, h_ref):
    mix = jnp.dot(ga_ref[...], w1_ref[...], preferred_element_type=f32)
    mix = mix + jnp.dot(cv_ref[...], w2_ref[...], preferred_element_type=f32)
    h_ref[...] = _layer_norm_rows(alpha * x_ref[...] + mix, g_ref[...], b_ref[...])


def _outproj(gla_out, conv_out, x2, w_o1, w_o2, g, b, alpha, tm=256):
    T, D = x2.shape
    W1 = gla_out.shape[1]
    W2 = conv_out.shape[1]
    return pl.pallas_call(
        functools.partial(_outproj_kernel, alpha),
        grid=(T // tm,),
        in_specs=[pl.BlockSpec((tm, W1), lambda i: (i, 0)),
                  pl.BlockSpec((tm, W2), lambda i: (i, 0)),
                  pl.BlockSpec((tm, D), lambda i: (i, 0)),
                  pl.BlockSpec((W1, D), lambda i: (0, 0)),
                  pl.BlockSpec((W2, D), lambda i: (0, 0)),
                  pl.BlockSpec((1, D), lambda i: (0, 0)),
                  pl.BlockSpec((1, D), lambda i: (0, 0))],
        out_specs=pl.BlockSpec((tm, D), lambda i: (i, 0)),
        out_shape=jax.ShapeDtypeStruct((T, D), f32),
        compiler_params=_cparams("parallel"),
    )(gla_out, conv_out, x2, w_o1, w_o2, g, b)


def _qproj_kernel(h_ref, w_ref, q_ref):
    q_ref[...] = jnp.dot(h_ref[...].astype(bf16), w_ref[...],
                         preferred_element_type=f32).astype(q_ref.dtype)


def _qproj(h2, w_q, tm=256):
    T, D = h2.shape
    N = w_q.shape[1]
    return pl.pallas_call(
        _qproj_kernel,
        grid=(T // tm,),
        in_specs=[pl.BlockSpec((tm, D), lambda i: (i, 0)),
                  pl.BlockSpec((D, N), lambda i: (0, 0))],
        out_specs=pl.BlockSpec((tm, N), lambda i: (i, 0)),
        out_shape=jax.ShapeDtypeStruct((T, N), bf16),
        compiler_params=_cparams("parallel"),
    )(h2, w_q)


def _topk_rows(s, k):
    R = s.shape[0]
    iota = lax.broadcasted_iota(i32, s.shape, 0)
    vals, idxs = [], []
    for _ in range(k):
        m = jnp.max(s, axis=0, keepdims=True)
        idx = jnp.min(jnp.where(s == m, iota, R), axis=0, keepdims=True)
        vals.append(m)
        idxs.append(idx)
        s = jnp.where(iota == idx, -jnp.inf, s)
    return vals, idxs


def _staircase_candidates(s1t, i1, s2t, i2):
    K = PEER_TOPK
    s1c, i1c = jnp.concatenate(s1t, axis=0), jnp.concatenate(i1, axis=0)
    s2c, i2c = jnp.concatenate(s2t, axis=0), jnp.concatenate(i2, axis=0)
    sub = lax.broadcasted_iota(i32, (SUBLANES, s1c.shape[1]), 0)
    vals, idxs = [], []
    a = 0
    while K // (a + 1) > 1:
        n_b = K // (a + 1)
        for b0 in range(0, n_b, SUBLANES):
            v = s1t[a] + s2c[b0:b0 + SUBLANES]
            if n_b - b0 < SUBLANES:
                v = jnp.where(sub < n_b - b0, v, -jnp.inf)
            vals.append(v)
            idxs.append(i1[a] * PEER_NKEYS + i2c[b0:b0 + SUBLANES])
        a += 1
    assert K - a == SUBLANES
    vals.append(s1c[a:K] + s2t[0])
    idxs.append(i1c[a:K] * PEER_NKEYS + i2[0])
    return jnp.concatenate(vals, axis=0), jnp.concatenate(idxs, axis=0)


def _bitonic_sort_groups(keys, vals):
    ng = len(keys)
    n = ng * SUBLANES
    sub = lax.broadcasted_iota(i32, keys[0].shape, 0)
    k = 2
    while k <= n:
        j = k // 2
        while j >= 1:
            if j >= SUBLANES:
                gj = j // SUBLANES
                for g in range(ng):
                    if g & gj:
                        continue
                    p = g | gj
                    asc = ((g * SUBLANES) & k) == 0
                    swap = keys[g] > keys[p] if asc else keys[g] < keys[p]
                    kg = jnp.where(swap, keys[p], keys[g])
                    kp = jnp.where(swap, keys[g], keys[p])
                    vg = jnp.where(swap, vals[p], vals[g])
                    vp = jnp.where(swap, vals[g], vals[p])
                    keys[g], keys[p], vals[g], vals[p] = kg, kp, vg, vp
            else:
                low = (sub & j) == 0
                for g in range(ng):
                    x, y = keys[g], vals[g]
                    px = jnp.where(low, pltpu.roll(x, SUBLANES - j, 0), pltpu.roll(x, j, 0))
                    py = jnp.where(low, pltpu.roll(y, SUBLANES - j, 0), pltpu.roll(y, j, 0))
                    if k >= SUBLANES:
                        asc = ((g * SUBLANES) & k) == 0
                        take_min = low if asc else jnp.logical_not(low)
                    else:
                        take_min = low == ((sub & k) == 0)
                    sel = (take_min & (px < x)) | (jnp.logical_not(take_min) & (px > x))
                    keys[g] = jnp.where(sel, px, x)
                    vals[g] = jnp.where(sel, py, y)
            j //= 2
        k *= 2
    return keys, vals


def _route_kernel(nba, nbv, q_ref, k1_ref, k2_ref, e_ref, eloca_ref, elocv_ref, gate_ref, st_ref):
    Tt = q_ref.shape[0]
    half = q_ref.shape[1] // PEER_HEADS // 2
    nt = (((1,), (1,)), ((), ()))
    e_rows, g_rows = [], []
    for h in range(PEER_HEADS):
        q1 = q_ref[:, (2 * h) * half:(2 * h + 1) * half]
        q2 = q_ref[:, (2 * h + 1) * half:(2 * h + 2) * half]
        s1 = lax.dot_general(k1_ref[h], q1, nt, preferred_element_type=f32)
        s2 = lax.dot_general(k2_ref[h], q2, nt, preferred_element_type=f32)
        s1t, i1 = _topk_rows(s1, PEER_TOPK)
        s2t, i2 = _topk_rows(s2, PEER_TOPK)
        cand, cidx = _staircase_candidates(s1t, i1, s2t, i2)
        iota = lax.broadcasted_iota(i32, cand.shape, 0)
        sc, ex = [], []
        for _ in range(PEER_TOPK):
            m = jnp.max(cand, axis=0, keepdims=True)
            pos = jnp.min(jnp.where(cand == m, iota, cand.shape[0]), axis=0, keepdims=True)
            hit = iota == pos
            ex.append(jnp.sum(jnp.where(hit, cidx, 0), axis=0, keepdims=True))
            sc.append(m)
            cand = jnp.where(hit, -jnp.inf, cand)
        p = [jnp.exp(s - sc[0]) for s in sc]
        denom = p[0]
        for t in p[1:]:
            denom = denom + t
        e_rows += ex
        g_rows += [t / denom for t in p]
    ng = PEER_PAIRS // SUBLANES
    keys = [jnp.concatenate(e_rows[g * SUBLANES:(g + 1) * SUBLANES], axis=0) for g in range(ng)]
    vals = [jnp.concatenate(g_rows[g * SUBLANES:(g + 1) * SUBLANES], axis=0) for g in range(ng)]
    keys, vals = _bitonic_sort_groups(keys, vals)
    e_tok = jnp.concatenate(keys, axis=0).T
    e_ref[...] = e_tok
    gate_ref[...] = jnp.concatenate(vals, axis=0).T
    lane = lax.broadcasted_iota(i32, (Tt, LANES), 1)
    starts = jnp.zeros((Tt, LANES), i32)
    for nb, eloc_ref, off in ((nba, eloca_ref, 0), (nbv, elocv_ref, PEER_ST_STRIDE)):
        bsz = (PEER_NKEYS * PEER_NKEYS) // nb
        for b in range(nb):
            eloc_ref[b] = jnp.clip(e_tok - b * bsz, 0, bsz - 1)
        for b in range(1, nb + 1):
            cnt = jnp.sum((e_tok < b * bsz).astype(i32), axis=1, keepdims=True)
            starts = jnp.where(lane == off + b, cnt, starts)
    st_ref[...] = starts


def _route(q, k1, k2, nba, nbv, tt=128):
    T, QW = q.shape
    H, NK, half = k1.shape
    return pl.pallas_call(
        functools.partial(_route_kernel, nba, nbv),
        grid=(T // tt,),
        in_specs=[pl.BlockSpec((tt, QW), lambda i: (i, 0)),
                  pl.BlockSpec((H, NK, half), lambda i: (0, 0, 0)),
                  pl.BlockSpec((H, NK, half), lambda i: (0, 0, 0))],
        out_specs=[pl.BlockSpec((tt, PEER_PAIRS), lambda i: (i, 0)),
                   pl.BlockSpec((nba, tt, PEER_PAIRS), lambda i: (0, i, 0)),
                   pl.BlockSpec((nbv, tt, PEER_PAIRS), lambda i: (0, i, 0)),
                   pl.BlockSpec((tt, PEER_PAIRS), lambda i: (i, 0)),
                   pl.BlockSpec((tt, LANES), lambda i: (i, 0))],
        out_shape=[jax.ShapeDtypeStruct((T, PEER_PAIRS), i32),
                   jax.ShapeDtypeStruct((nba, T, PEER_PAIRS), i32),
                   jax.ShapeDtypeStruct((nbv, T, PEER_PAIRS), i32),
                   jax.ShapeDtypeStruct((T, PEER_PAIRS), f32),
                   jax.ShapeDtypeStruct((T, LANES), i32)],
        compiler_params=_cparams("parallel"),
    )(q, k1, k2)


PEER_ST_STRIDE = SUBLANES


def _chunk_range(st_smem, idx, chunk):
    s0 = st_smem[idx]
    s1 = st_smem[idx + 1]
    sh = int(math.log2(chunk))
    cs = lax.shift_right_logical(s0, sh)
    ce = jnp.where(s1 > s0, lax.shift_right_logical(s1 + (chunk - 1), sh), cs)
    return s0, s1, cs, ce


def _prefetch_lists(step, nsteps, copies_for):
    slot = step & 1

    @pl.when(step == 0)
    def _():
        for c in copies_for(step, slot):
            c.start()

    @pl.when(step + 1 < nsteps)
    def _():
        for c in copies_for(step + 1, 1 - slot):
            c.start()

    for c in copies_for(step, slot):
        c.wait()
    return slot


def _gelu_exact(x):
    return 0.5 * x * (1.0 + lax.erf(x * (2.0 ** -0.5)))


def _fold(a, b, s, sub):
    ta = a + pltpu.roll(a, s, 0)
    tb = b + pltpu.roll(b, SUBLANES - s, 0)
    return jnp.where((sub & s) != 0, ta, tb)


def _fold_first(vs, sub):
    upper = (sub & 4) != 0
    return tuple(jnp.where(upper, vs[r + 4], vs[r]) + pltpu.roll(jnp.where(upper, vs[r], vs[r + 4]), 4, 0)
                 for r in range(4))


def _fold_rest(ms, sub):
    return _fold(_fold(ms[3], ms[1], 2, sub), _fold(ms[2], ms[0], 2, sub), 1, sub)


def _act_kernel(nba, nbv, eloc_hbm, st_hbm, h_ref, u_ref, e_ref, gate_ref, w_ref,
                e_smem, st_smem, stage_ref, act_ref, sem):
    i = pl.program_id(0)
    b = pl.program_id(1)
    Tk = h_ref.shape[0]
    T = Tk * pl.num_programs(0)
    n = Tk * PEER_PAIRS
    ns = Tk * PEER_ST_STRIDE
    lg = int(math.log2(nba))

    def copies_for(step, slot):
        si = lax.shift_right_logical(step, lg)
        sb = step & (nba - 1)
        src_e = eloc_hbm.at[pl.ds(pl.multiple_of((sb * T + si * Tk) * PEER_PAIRS, n), n)]
        src_s = st_hbm.at[pl.ds(pl.multiple_of(si * ns, ns), ns)]
        return (pltpu.make_async_copy(src_e, e_smem.at[pl.ds(pl.multiple_of(slot * n, n), n)], sem.at[slot, 0]),
                pltpu.make_async_copy(src_s, st_smem.at[pl.ds(pl.multiple_of(slot * ns, ns), ns)], sem.at[slot, 1]))

    slot = _prefetch_lists(i * nba + b, pl.num_programs(0) * nba, copies_for)
    eoff = slot * n
    soff = slot * ns + b
    sub = lax.broadcasted_iota(i32, (SUBLANES, LANES), 0)

    def token(t, carry):
        s0, s1, cs, ce = _chunk_range(st_smem, soff + t * PEER_ST_STRIDE, PEER_CHUNK)
        hrow = h_ref[t]
        base = eoff + t * PEER_PAIRS

        def flush(c, folded):
            first = pl.multiple_of(c * PEER_CHUNK, PEER_CHUNK)
            part = _fold_rest(folded, sub)
            pos = sub + first
            pltpu.store(stage_ref.at[t, pl.ds(first, PEER_CHUNK), :], part,
                        mask=(pos >= s0) & (pos < s1))

        def expert_ids(c):
            first = base + jnp.minimum(c, PEER_NCHUNK - 1) * PEER_CHUNK
            return tuple(e_smem[first + k] for k in range(PEER_CHUNK))

        def chunk(c, pending):
            c_prev, folded_prev, ids = pending
            halves = []
            for k in range(PEER_CHUNK):
                pr = hrow * u_ref[ids[k]].astype(f32)
                halves.append(pr[:SUBLANES] + pr[SUBLANES:])
            flush(c_prev, folded_prev)
            return c, _fold_first(halves, sub), expert_ids(c + 1)

        zero = jnp.zeros((SUBLANES, LANES), f32)
        pending = (jnp.minimum(cs, PEER_NCHUNK - 1), (zero,) * (PEER_CHUNK // 2), expert_ids(cs))
        c_last, folded_last, _ = lax.fori_loop(cs, ce, chunk, pending)
        flush(c_last, folded_last)
        return carry

    lax.fori_loop(0, Tk, token, 0)

    @pl.when(b == nba - 1)
    def _():
        unroll = 4

        def reduce_tokens(g, carry):
            for k in range(unroll):
                t = g * unroll + k
                act_ref[pl.ds(t, 1), :] = jnp.sum(stage_ref[t].T, axis=0, keepdims=True)
            return carry

        lax.fori_loop(0, Tk // unroll, reduce_tokens, 0)
        w = gate_ref[...] * _gelu_exact(act_ref[...])
        blk = lax.shift_right_logical(e_ref[...], int(math.log2(PEER_NKEYS * PEER_NKEYS // nbv)))
        for bb in range(nbv):
            w_ref[bb] = jnp.where(blk == bb, w, 0.0)


def _peer_act(eloc, starts, h3, u3, e_tok, gate, nba, nbv, tk):
    T = h3.shape[0]
    E = u3.shape[0]
    bsz = E // nba
    rowblk = h3.shape[1:]
    return pl.pallas_call(
        functools.partial(_act_kernel, nba, nbv),
        grid=(T // tk, nba),
        in_specs=[pl.BlockSpec(memory_space=pl.ANY),
                  pl.BlockSpec(memory_space=pl.ANY),
                  pl.BlockSpec((tk,) + rowblk, lambda i, b: (i, 0, 0)),
                  pl.BlockSpec((bsz,) + rowblk, lambda i, b: (b, 0, 0)),
                  pl.BlockSpec((tk, PEER_PAIRS), lambda i, b: (i, 0)),
                  pl.BlockSpec((tk, PEER_PAIRS), lambda i, b: (i, 0))],
        out_specs=pl.BlockSpec((nbv, tk, PEER_PAIRS), lambda i, b: (0, i, 0)),
        out_shape=jax.ShapeDtypeStruct((nbv, T, PEER_PAIRS), f32),
        scratch_shapes=[pltpu.SMEM((2 * tk * PEER_PAIRS,), i32),
                        pltpu.SMEM((2 * tk * PEER_ST_STRIDE,), i32),
                        pltpu.VMEM((tk, PEER_PAIRS, LANES), f32),
                        pltpu.VMEM((tk, PEER_PAIRS), f32),
                        pltpu.SemaphoreType.DMA((2, 2))],
        compiler_params=_cparams("arbitrary", "arbitrary"),
    )(eloc, starts, h3, u3, e_tok, gate)


PEER_VAL_CHUNK = 4 * PEER_CHUNK
PEER_VAL_NACC = 4


def _val_kernel(eloc_hbm, st_hbm, w_hbm, v_hbm, o_ref, e_smem, st_smem, w_smem, v_ref, sem, vsem):
    b = pl.program_id(0)
    i = pl.program_id(1)
    ni = pl.num_programs(1)
    Tk = o_ref.shape[1]
    T = Tk * ni
    n = Tk * PEER_PAIRS
    ns = Tk * PEER_ST_STRIDE
    bsz = v_ref.shape[0]

    @pl.when(i == 0)
    def _():
        cp = pltpu.make_async_copy(v_hbm.at[pl.ds(b * bsz, bsz)], v_ref, vsem)
        cp.start()
        cp.wait()

    def copies_for(step, slot):
        sb = step // ni
        si = step - sb * ni
        lists = pl.ds(pl.multiple_of((sb * T + si * Tk) * PEER_PAIRS, n), n)
        dst = pl.ds(pl.multiple_of(slot * n, n), n)
        return (pltpu.make_async_copy(eloc_hbm.at[lists], e_smem.at[dst], sem.at[slot, 0]),
                pltpu.make_async_copy(st_hbm.at[pl.ds(pl.multiple_of(si * ns, ns), ns)],
                                      st_smem.at[pl.ds(pl.multiple_of(slot * ns, ns), ns)], sem.at[slot, 1]),
                pltpu.make_async_copy(w_hbm.at[lists], w_smem.at[dst], sem.at[slot, 2]))

    slot = _prefetch_lists(b * ni + i, pl.num_programs(0) * ni, copies_for)
    eoff = slot * n
    soff = slot * ns + b

    def token(t, carry):
        _, _, cs, ce = _chunk_range(st_smem, soff + t * PEER_ST_STRIDE, PEER_VAL_CHUNK)
        base = eoff + t * PEER_PAIRS

        def chunk(c, accs):
            first = base + c * PEER_VAL_CHUNK
            accs = list(accs)
            for k in range(PEER_VAL_CHUNK):
                term = w_smem[first + k] * v_ref[e_smem[first + k]].astype(f32)
                accs[k % PEER_VAL_NACC] = accs[k % PEER_VAL_NACC] + term
            return tuple(accs)

        zero = jnp.zeros(o_ref.shape[2:], f32)
        accs = lax.fori_loop(cs, ce, chunk, (zero,) * PEER_VAL_NACC)
        o_ref[0, t] = (accs[0] + accs[1]) + (accs[2] + accs[3])
        return carry

    lax.fori_loop(0, Tk, token, 0)


def _peer_val(eloc, starts, w, v3, nb, tk):
    T = eloc.shape[0] // (nb * PEER_PAIRS)
    E = v3.shape[0]
    bsz = E // nb
    rowblk = v3.shape[1:]
    return pl.pallas_call(
        _val_kernel,
        grid=(nb, T // tk),
        in_specs=[pl.BlockSpec(memory_space=pl.ANY),
                  pl.BlockSpec(memory_space=pl.ANY),
                  pl.BlockSpec(memory_space=pl.ANY),
                  pl.BlockSpec(memory_space=pl.ANY)],
        out_specs=pl.BlockSpec((1, tk) + rowblk, lambda b, i: (b, i, 0, 0)),
        out_shape=jax.ShapeDtypeStruct((nb, T) + rowblk, f32),
        scratch_shapes=[pltpu.SMEM((2 * tk * PEER_PAIRS,), i32),
                        pltpu.SMEM((2 * tk * PEER_ST_STRIDE,), i32),
                        pltpu.SMEM((2 * tk * PEER_PAIRS,), f32),
                        pltpu.VMEM((bsz,) + rowblk, v3.dtype),
                        pltpu.SemaphoreType.DMA((2, 3)),
                        pltpu.SemaphoreType.DMA(())],
        compiler_params=_cparams("arbitrary", "arbitrary"),
    )(eloc, starts, w, v3)


def _ln2_kernel(alpha, h_ref, f_ref, g_ref, b_ref, o_ref):
    o_ref[...] = _layer_norm_rows(alpha * h_ref[...] + jnp.sum(f_ref[...], axis=0), g_ref[...], b_ref[...])


def _ln2(h2, ffn, g, b, alpha, tm=512):
    T, D = h2.shape
    return pl.pallas_call(
        functools.partial(_ln2_kernel, alpha),
        grid=(T // tm,),
        in_specs=[pl.BlockSpec((tm, D), lambda i: (i, 0)),
                  pl.BlockSpec((ffn.shape[0], tm, D), lambda i: (0, i, 0)),
                  pl.BlockSpec((1, D), lambda i: (0, 0)),
                  pl.BlockSpec((1, D), lambda i: (0, 0))],
        out_specs=pl.BlockSpec((tm, D), lambda i: (i, 0)),
        out_shape=jax.ShapeDtypeStruct((T, D), f32),
        compiler_params=_cparams("parallel"),
    )(h2, ffn, g, b)


PEER_ACT_NBLOCKS = 4
PEER_VAL_NBLOCKS = 2
PEER_ACT_TOKENS = 256
PEER_VAL_TOKENS = 256


def _layer(x, w_in, w_a2, b_a, gla_norm_g, conv_w, conv_b, conv_ln_g, conv_ln_b, w_out,
           ln1_g, ln1_b, w_q, sub_k1, sub_k2, u_tab, v_tab, ln2_g, ln2_b, alpha):
    B, L, D = x.shape
    T = B * L
    kw = GLA_HEADS * GLA_DK
    vw = GLA_HEADS * GLA_DV
    split_r = 2 * kw + 2 * vw
    split_a = split_r + GLA_GATE_RANK
    cw = conv_w.shape[1]
    x2 = x.reshape(T, D)
    w_main = jnp.concatenate([w_in[:, :split_r], w_in[:, split_a:]], axis=1).astype(bf16)
    w_a = jnp.pad(w_in[:, split_r:split_a], ((0, 0), (0, LANES - GLA_GATE_RANK))).astype(bf16)
    w_a2p = jnp.pad(w_a2, ((0, LANES - GLA_GATE_RANK), (0, 0)))
    conv_wp = jnp.pad(conv_w, ((0, CONV_HALO - CONV_KERNEL), (0, 0)))
    proj, a_lr = _inproj(x2, w_main, w_a)
    proj3 = proj.reshape(B, L, -1)
    gla_out = _gla(proj3, a_lr.reshape(B, L, LANES), w_a2p, b_a.reshape(1, kw),
                   gla_norm_g.reshape(1, vw))
    conv_out = _conv(proj3, conv_wp, conv_b.reshape(1, cw), conv_ln_g.reshape(1, cw),
                     conv_ln_b.reshape(1, cw), val_blk=split_r // cw)
    h2 = _outproj(gla_out.reshape(T, vw), conv_out.reshape(T, cw), x2,
                  w_out[:vw].astype(bf16), w_out[vw:].astype(bf16),
                  ln1_g.reshape(1, D), ln1_b.reshape(1, D), alpha)
    q = _qproj(h2, w_q.astype(bf16))
    nba, nbv = PEER_ACT_NBLOCKS, PEER_VAL_NBLOCKS
    e_tok, eloc_a, eloc_v, gate, starts = _route(q, sub_k1.astype(bf16), sub_k2.astype(bf16), nba, nbv)
    rows = D // LANES
    u3 = u_tab.astype(bf16).reshape(-1, rows, LANES)
    v3 = v_tab.astype(bf16).reshape(-1, rows, LANES)
    h3 = h2.reshape(T, rows, LANES)
    starts_a = starts[:, :PEER_ST_STRIDE].reshape(-1)
    starts_v = starts[:, PEER_ST_STRIDE:2 * PEER_ST_STRIDE].reshape(-1)
    w = _peer_act(eloc_a.reshape(-1), starts_a, h3, u3, e_tok, gate, nba, nbv, PEER_ACT_TOKENS)
    ffn = _peer_val(eloc_v.reshape(-1), starts_v, w.reshape(-1), v3, nbv, PEER_VAL_TOKENS)
    return _ln2(h2, ffn.reshape(nbv, T, D), ln2_g.reshape(1, D), ln2_b.reshape(1, D), alpha).reshape(B, L, D)


def kernel(x, w_in, w_a2, b_a, gla_norm_g, conv_w, conv_b, conv_ln_g, conv_ln_b, w_out, ln1_g, ln1_b, w_q, sub_k1, sub_k2, u_tab, v_tab, ln2_g, ln2_b):
    depth = w_in.shape[0]
    alpha = (2.0 * depth) ** 0.25
    for l in range(depth):
        x = _layer(x, w_in[l], w_a2[l], b_a[l], gla_norm_g[l], conv_w[l], conv_b[l], conv_ln_g[l],
                   conv_ln_b[l], w_out[l], ln1_g[l], ln1_b[l], w_q[l], sub_k1[l], sub_k2[l],
                   u_tab[l], v_tab[l], ln2_g[l], ln2_b[l], alpha)
    return x
```

```python
import functools
import math

import jax
import jax.numpy as jnp
from jax import lax
from jax.experimental import pallas as pl
from jax.experimental.pallas import tpu as pltpu

f32 = jnp.float32
bf16 = jnp.bfloat16
i32 = jnp.int32

LANES = 128
SUBLANES = 8
VMEM_LIMIT_BYTES = 60 * 1024 * 1024

GLA_HEADS = 4
GLA_DK = 128
GLA_DV = 256
GLA_GATE_RANK = 16
GLA_TAU = 16.0
GLA_CHUNK = 64
CONV_KERNEL = 31
CONV_HALO = 32
PEER_HEADS = 8
PEER_NKEYS = 128
PEER_TOPK = 16
PEER_PAIRS = PEER_HEADS * PEER_TOPK
PEER_CHUNK = SUBLANES
PEER_NCHUNK = PEER_PAIRS // PEER_CHUNK
PEER_TABLE_ROWS = SUBLANES
LN_EPS = 1e-5


def _cparams(*sem):
    return pltpu.CompilerParams(dimension_semantics=sem, vmem_limit_bytes=VMEM_LIMIT_BYTES)


def _layer_norm_rows(x, g, b=None):
    mu = jnp.mean(x, axis=-1, keepdims=True)
    xc = x - mu
    var = jnp.mean(xc * xc, axis=-1, keepdims=True)
    out = xc * lax.rsqrt(var + LN_EPS) * g
    if b is not None:
        out = out + b
    return out


def _silu(x):
    return x * jax.nn.sigmoid(x)


def _inproj_kernel(x_ref, w_ref, wa_ref, o_ref, a_ref, xb_ref):
    @pl.when(pl.program_id(1) == 0)
    def _():
        xb = x_ref[...].astype(bf16)
        xb_ref[...] = xb
        a_ref[...] = jnp.dot(xb, wa_ref[...], preferred_element_type=f32)

    o_ref[...] = jnp.dot(xb_ref[...], w_ref[...], preferred_element_type=f32)


def _inproj(x2, w_main, w_a, tm=512, tn=1024):
    T, D = x2.shape
    N = w_main.shape[1]
    return pl.pallas_call(
        _inproj_kernel,
        grid=(T // tm, N // tn),
        in_specs=[pl.BlockSpec((tm, D), lambda i, j: (i, 0)),
                  pl.BlockSpec((D, tn), lambda i, j: (0, j)),
                  pl.BlockSpec((D, LANES), lambda i, j: (0, 0))],
        out_specs=[pl.BlockSpec((tm, tn), lambda i, j: (i, j)),
                   pl.BlockSpec((tm, LANES), lambda i, j: (i, 0))],
        out_shape=[jax.ShapeDtypeStruct((T, N), f32), jax.ShapeDtypeStruct((T, LANES), f32)],
        scratch_shapes=[pltpu.VMEM((tm, D), bf16)],
        compiler_params=_cparams("parallel", "arbitrary"),
    )(x2, w_main, w_a)


def _gla_kernel(q_ref, k_ref, v_ref, r_ref, a_ref, wa2_ref, ba_ref, g_ref, o_ref, st_ref):
    C = GLA_CHUNK
    L = q_ref.shape[1]

    @pl.when(pl.program_id(2) == 0)
    def _():
        st_ref[...] = jnp.zeros_like(st_ref)

    row = lax.broadcasted_iota(i32, (C, C), 0)
    col = lax.broadcasted_iota(i32, (C, C), 1)
    causal = row >= col
    tril = causal.astype(f32)
    scale = GLA_DK ** -0.5
    DK, DV = GLA_DK, GLA_DV
    heads = q_ref.shape[2] // DK
    nt = (((1,), (1,)), ((), ()))

    def chunk(n, carry):
        rows = pl.ds(pl.multiple_of(n * C, C), C)
        a_lr = a_ref[0, rows, :]
        for hh in range(heads):
            kc = slice(hh * DK, (hh + 1) * DK)
            vc = slice(hh * DV, (hh + 1) * DV)
            z = jnp.dot(a_lr, wa2_ref[:, kc], preferred_element_type=f32,
                        precision=lax.Precision.HIGHEST) + ba_ref[:, kc]
            log_a = (jnp.minimum(z, 0.0) - jnp.log1p(jnp.exp(-jnp.abs(z)))) / GLA_TAU
            b = jnp.dot(tril, log_a, preferred_element_type=f32, precision=lax.Precision.HIGHEST)
            b_ref = b[C // 2:C // 2 + 1, :]
            b_last = b[C - 1:C, :]
            q = q_ref[0, rows, kc] * scale
            k = k_ref[0, rows, kc]
            v = v_ref[0, rows, vc].astype(bf16)
            q_in = (q * jnp.exp(b - b_ref)).astype(bf16)
            k_in = (k * jnp.exp(b_ref - b)).astype(bf16)
            scores = lax.dot_general(q_in, k_in, nt, preferred_element_type=f32)
            scores = jnp.where(causal, scores, 0.0)
            o = jnp.dot(scores.astype(bf16), v, preferred_element_type=f32)
            st = st_ref[hh]
            q_dec = (q * jnp.exp(b)).astype(bf16)
            o = o + lax.dot_general(q_dec, st.astype(bf16), nt, preferred_element_type=f32)
            k_dec = (k * jnp.exp(b_last - b)).astype(bf16)
            kv_t = lax.dot_general(v, k_dec, (((0,), (0,)), ((), ())), preferred_element_type=f32)
            st_ref[hh] = st * jnp.exp(b_last) + kv_t
            o = _layer_norm_rows(o, g_ref[:, vc])
            o_ref[0, rows, vc] = (o * _silu(r_ref[0, rows, vc])).astype(o_ref.dtype)
        return carry

    lax.fori_loop(0, L // C, chunk, 0)


def _gla(proj3, a3, w_a2p, b_a, gla_g, heads_per_step=4, tl=1024):
    B, L, _ = proj3.shape
    H, hs = GLA_HEADS, heads_per_step
    tl = min(tl, L)
    DK, DV = hs * GLA_DK, hs * GLA_DV
    kq = H // hs
    kv = 2 * kq * DK // DV
    kr = kv + H // hs
    return pl.pallas_call(
        _gla_kernel,
        grid=(B, H // hs, L // tl),
        in_specs=[pl.BlockSpec((1, tl, DK), lambda b, h, l: (b, l, h)),
                  pl.BlockSpec((1, tl, DK), lambda b, h, l: (b, l, kq + h)),
                  pl.BlockSpec((1, tl, DV), lambda b, h, l: (b, l, kv + h)),
                  pl.BlockSpec((1, tl, DV), lambda b, h, l: (b, l, kr + h)),
                  pl.BlockSpec((1, tl, LANES), lambda b, h, l: (b, l, 0)),
                  pl.BlockSpec((LANES, DK), lambda b, h, l: (0, h)),
                  pl.BlockSpec((1, DK), lambda b, h, l: (0, h)),
                  pl.BlockSpec((1, DV), lambda b, h, l: (0, h))],
        out_specs=pl.BlockSpec((1, tl, DV), lambda b, h, l: (b, l, h)),
        out_shape=jax.ShapeDtypeStruct((B, L, H * GLA_DV), bf16),
        scratch_shapes=[pltpu.VMEM((hs, GLA_DV, GLA_DK), f32)],
        compiler_params=_cparams("parallel", "parallel", "arbitrary"),
    )(proj3, proj3, proj3, proj3, a3, w_a2p, b_a, gla_g)


def _conv_kernel(val_ref, gate_ref, w_ref, cb_ref, g_ref, b_ref, o_ref, glu_ref, acc_ref):
    tl = val_ref.shape[1]
    W = val_ref.shape[2]
    rc = 64

    @pl.when(pl.program_id(1) == 0)
    def _():
        glu_ref[0:CONV_HALO, :] = jnp.zeros((CONV_HALO, W), f32)

    @pl.when(pl.program_id(1) != 0)
    def _():
        glu_ref[0:CONV_HALO, :] = glu_ref[tl:tl + CONV_HALO, :]

    glu_ref[CONV_HALO:CONV_HALO + tl, :] = val_ref[0] * jax.nn.sigmoid(gate_ref[0])
    lead = CONV_HALO - (CONV_KERNEL - 1)

    def strip(s, carry):
        cols = pl.ds(pl.multiple_of(s * LANES, LANES), LANES)
        wj = w_ref[:, cols]
        bias = cb_ref[:, cols]
        for r0 in range(0, tl, rc):
            acc = jnp.zeros((rc, LANES), f32) + bias
            for j in range(CONV_KERNEL):
                acc = acc + glu_ref[pl.ds(r0 + lead + j, rc), cols] * wj[j:j + 1, :]
            acc_ref[pl.ds(r0, rc), cols] = acc
        return carry

    lax.fori_loop(0, W // LANES, strip, 0)
    y = _layer_norm_rows(acc_ref[...], g_ref[...], b_ref[...])
    o_ref[0] = _silu(y).astype(o_ref.dtype)


def _conv(proj3, conv_wp, conv_b, ln_g, ln_b, val_blk, tl=256):
    B, L, _ = proj3.shape
    W = conv_wp.shape[1]
    return pl.pallas_call(
        _conv_kernel,
        grid=(B, L // tl),
        in_specs=[pl.BlockSpec((1, tl, W), lambda b, l: (b, l, val_blk)),
                  pl.BlockSpec((1, tl, W), lambda b, l: (b, l, val_blk + 1)),
                  pl.BlockSpec(conv_wp.shape, lambda b, l: (0, 0)),
                  pl.BlockSpec((1, W), lambda b, l: (0, 0)),
                  pl.BlockSpec((1, W), lambda b, l: (0, 0)),
                  pl.BlockSpec((1, W), lambda b, l: (0, 0))],
        out_specs=pl.BlockSpec((1, tl, W), lambda b, l: (b, l, 0)),
        out_shape=jax.ShapeDtypeStruct((B, L, W), bf16),
        scratch_shapes=[pltpu.VMEM((tl + CONV_HALO, W), f32), pltpu.VMEM((tl, W), f32)],
        compiler_params=_cparams("parallel", "arbitrary"),
    )(proj3, proj3, conv_wp, conv_b, ln_g, ln_b)


def _outproj_kernel(alpha, ga_ref, cv_ref, x_ref, w1_ref, w2_ref, g_ref, b_ref, h_ref):
    mix = jnp.dot(ga_ref[...], w1_ref[...], preferred_element_type=f32)
    mix = mix + jnp.dot(cv_ref[...], w2_ref[...], preferred_element_type=f32)
    h_ref[...] = _layer_norm_rows(alpha * x_ref[...] + mix, g_ref[...], b_ref[...])


def _outproj(gla_out, conv_out, x2, w_o1, w_o2, g, b, alpha, tm=256):
    T, D = x2.shape
    W1 = gla_out.shape[1]
    W2 = conv_out.shape[1]
    return pl.pallas_call(
        functools.partial(_outproj_kernel, alpha),
        grid=(T // tm,),
        in_specs=[pl.BlockSpec((tm, W1), lambda i: (i, 0)),
                  pl.BlockSpec((tm, W2), lambda i: (i, 0)),
                  pl.BlockSpec((tm, D), lambda i: (i, 0)),
                  pl.BlockSpec((W1, D), lambda i: (0, 0)),
                  pl.BlockSpec((W2, D), lambda i: (0, 0)),
                  pl.BlockSpec((1, D), lambda i: (0, 0)),
                  pl.BlockSpec((1, D), lambda i: (0, 0))],
        out_specs=pl.BlockSpec((tm, D), lambda i: (i, 0)),
        out_shape=jax.ShapeDtypeStruct((T, D), f32),
        compiler_params=_cparams("parallel"),
    )(gla_out, conv_out, x2, w_o1, w_o2, g, b)


def _qproj_kernel(h_ref, w_ref, q_ref):
    q_ref[...] = jnp.dot(h_ref[...].astype(bf16), w_ref[...],
                         preferred_element_type=f32).astype(q_ref.dtype)


def _qproj(h2, w_q, tm=256):
    T, D = h2.shape
    N = w_q.shape[1]
    return pl.pallas_call(
        _qproj_kernel,
        grid=(T // tm,),
        in_specs=[pl.BlockSpec((tm, D), lambda i: (i, 0)),
                  pl.BlockSpec((D, N), lambda i: (0, 0))],
        out_specs=pl.BlockSpec((tm, N), lambda i: (i, 0)),
        out_shape=jax.ShapeDtypeStruct((T, N), bf16),
        compiler_params=_cparams("parallel"),
    )(h2, w_q)


def _topk_rows(s, k):
    R = s.shape[0]
    iota = lax.broadcasted_iota(i32, s.shape, 0)
    vals, idxs = [], []
    for _ in range(k):
        m = jnp.max(s, axis=0, keepdims=True)
        idx = jnp.min(jnp.where(s == m, iota, R), axis=0, keepdims=True)
        vals.append(m)
        idxs.append(idx)
        s = jnp.where(iota == idx, -jnp.inf, s)
    return vals, idxs


def _staircase_candidates(s1t, i1, s2t, i2):
    K = PEER_TOPK
    s1c, i1c = jnp.concatenate(s1t, axis=0), jnp.concatenate(i1, axis=0)
    s2c, i2c = jnp.concatenate(s2t, axis=0), jnp.concatenate(i2, axis=0)
    sub = lax.broadcasted_iota(i32, (SUBLANES, s1c.shape[1]), 0)
    vals, idxs = [], []
    a = 0
    while K // (a + 1) > 1:
        n_b = K // (a + 1)
        for b0 in range(0, n_b, SUBLANES):
            v = s1t[a] + s2c[b0:b0 + SUBLANES]
            if n_b - b0 < SUBLANES:
                v = jnp.where(sub < n_b - b0, v, -jnp.inf)
            vals.append(v)
            idxs.append(i1[a] * PEER_NKEYS + i2c[b0:b0 + SUBLANES])
        a += 1
    assert K - a == SUBLANES
    vals.append(s1c[a:K] + s2t[0])
    idxs.append(i1c[a:K] * PEER_NKEYS + i2[0])
    return jnp.concatenate(vals, axis=0), jnp.concatenate(idxs, axis=0)


def _bitonic_sort_groups(keys, vals):
    ng = len(keys)
    n = ng * SUBLANES
    sub = lax.broadcasted_iota(i32, keys[0].shape, 0)
    k = 2
    while k <= n:
        j = k // 2
        while j >= 1:
            if j >= SUBLANES:
                gj = j // SUBLANES
                for g in range(ng):
                    if g & gj:
                        continue
                    p = g | gj
                    asc = ((g * SUBLANES) & k) == 0
                    swap = keys[g] > keys[p] if asc else keys[g] < keys[p]
                    kg = jnp.where(swap, keys[p], keys[g])
                    kp = jnp.where(swap, keys[g], keys[p])
                    vg = jnp.where(swap, vals[p], vals[g])
                    vp = jnp.where(swap, vals[g], vals[p])
                    keys[g], keys[p], vals[g], vals[p] = kg, kp, vg, vp
            else:
                low = (sub & j) == 0
                for g in range(ng):
                    x, y = keys[g], vals[g]
                    px = jnp.where(low, pltpu.roll(x, SUBLANES - j, 0), pltpu.roll(x, j, 0))
                    py = jnp.where(low, pltpu.roll(y, SUBLANES - j, 0), pltpu.roll(y, j, 0))
                    if k >= SUBLANES:
                        asc = ((g * SUBLANES) & k) == 0
                        take_min = low if asc else jnp.logical_not(low)
                    else:
                        take_min = low == ((sub & k) == 0)
                    sel = (take_min & (px < x)) | (jnp.logical_not(take_min) & (px > x))
                    keys[g] = jnp.where(sel, px, x)
                    vals[g] = jnp.where(sel, py, y)
            j //= 2
        k *= 2
    return keys, vals


def _route_kernel(nba, nbv, q_ref, k1_ref, k2_ref, e_ref, eloca_ref, elocv_ref, gate_ref, st_ref):
    Tt = q_ref.shape[0]
    half = q_ref.shape[1] // PEER_HEADS // 2
    nt = (((1,), (1,)), ((), ()))
    e_rows, g_rows = [], []
    for h in range(PEER_HEADS):
        q1 = q_ref[:, (2 * h) * half:(2 * h + 1) * half]
        q2 = q_ref[:, (2 * h + 1) * half:(2 * h + 2) * half]
        s1 = lax.dot_general(k1_ref[h], q1, nt, preferred_element_type=f32)
        s2 = lax.dot_general(k2_ref[h], q2, nt, preferred_element_type=f32)
        s1t, i1 = _topk_rows(s1, PEER_TOPK)
        s2t, i2 = _topk_rows(s2, PEER_TOPK)
        cand, cidx = _staircase_candidates(s1t, i1, s2t, i2)
        iota = lax.broadcasted_iota(i32, cand.shape, 0)
        sc, ex = [], []
        for _ in range(PEER_TOPK):
            m = jnp.max(cand, axis=0, keepdims=True)
            pos = jnp.min(jnp.where(cand == m, iota, cand.shape[0]), axis=0, keepdims=True)
            hit = iota == pos
            ex.append(jnp.sum(jnp.where(hit, cidx, 0), axis=0, keepdims=True))
            sc.append(m)
            cand = jnp.where(hit, -jnp.inf, cand)
        p = [jnp.exp(s - sc[0]) for s in sc]
        denom = p[0]
        for t in p[1:]:
            denom = denom + t
        e_rows += ex
        g_rows += [t / denom for t in p]
    ng = PEER_PAIRS // SUBLANES
    keys = [jnp.concatenate(e_rows[g * SUBLANES:(g + 1) * SUBLANES], axis=0) for g in range(ng)]
    vals = [jnp.concatenate(g_rows[g * SUBLANES:(g + 1) * SUBLANES], axis=0) for g in range(ng)]
    keys, vals = _bitonic_sort_groups(keys, vals)
    e_tok = jnp.concatenate(keys, axis=0).T
    e_ref[...] = e_tok
    gate_ref[...] = jnp.concatenate(vals, axis=0).T
    lane = lax.broadcasted_iota(i32, (Tt, LANES), 1)
    starts = jnp.zeros((Tt, LANES), i32)
    for nb, eloc_ref, off in ((nba, eloca_ref, 0), (nbv, elocv_ref, PEER_ST_STRIDE)):
        bsz = (PEER_NKEYS * PEER_NKEYS) // nb
        for b in range(nb):
            eloc_ref[b] = jnp.clip(e_tok - b * bsz, 0, bsz - 1) * PEER_TABLE_ROWS
        for b in range(1, nb + 1):
            cnt = jnp.sum((e_tok < b * bsz).astype(i32), axis=1, keepdims=True)
            starts = jnp.where(lane == off + b, cnt, starts)
    st_ref[...] = starts


def _route(q, k1, k2, nba, nbv, tt=128):
    T, QW = q.shape
    H, NK, half = k1.shape
    return pl.pallas_call(
        functools.partial(_route_kernel, nba, nbv),
        grid=(T // tt,),
        in_specs=[pl.BlockSpec((tt, QW), lambda i: (i, 0)),
                  pl.BlockSpec((H, NK, half), lambda i: (0, 0, 0)),
                  pl.BlockSpec((H, NK, half), lambda i: (0, 0, 0))],
        out_specs=[pl.BlockSpec((tt, PEER_PAIRS), lambda i: (i, 0)),
                   pl.BlockSpec((nba, tt, PEER_PAIRS), lambda i: (0, i, 0)),
                   pl.BlockSpec((nbv, tt, PEER_PAIRS), lambda i: (0, i, 0)),
                   pl.BlockSpec((tt, PEER_PAIRS), lambda i: (i, 0)),
                   pl.BlockSpec((tt, LANES), lambda i: (i, 0))],
        out_shape=[jax.ShapeDtypeStruct((T, PEER_PAIRS), i32),
                   jax.ShapeDtypeStruct((nba, T, PEER_PAIRS), i32),
                   jax.ShapeDtypeStruct((nbv, T, PEER_PAIRS), i32),
                   jax.ShapeDtypeStruct((T, PEER_PAIRS), f32),
                   jax.ShapeDtypeStruct((T, LANES), i32)],
        compiler_params=_cparams("parallel"),
    )(q, k1, k2)


PEER_ST_STRIDE = SUBLANES


def _chunk_range(st_smem, idx, chunk):
    s0 = st_smem[idx]
    s1 = st_smem[idx + 1]
    sh = int(math.log2(chunk))
    cs = lax.shift_right_logical(s0, sh)
    ce = jnp.where(s1 > s0, lax.shift_right_logical(s1 + (chunk - 1), sh), cs)
    return s0, s1, cs, ce


def _prefetch_lists(step, nsteps, copies_for):
    slot = step & 1

    @pl.when(step == 0)
    def _():
        for c in copies_for(step, slot):
            c.start()

    @pl.when(step + 1 < nsteps)
    def _():
        for c in copies_for(step + 1, 1 - slot):
            c.start()

    for c in copies_for(step, slot):
        c.wait()
    return slot


def _table_row(tab_ref, row):
    words = tab_ref[pl.ds(pl.multiple_of(row, PEER_TABLE_ROWS), PEER_TABLE_ROWS), :]
    return pltpu.bitcast(words, bf16).astype(f32)


def _pack_table(tab):
    E, D = tab.shape
    pairs = tab.astype(bf16).reshape(E, D // (2 * LANES), 2, LANES).transpose(0, 1, 3, 2)
    return lax.bitcast_convert_type(pairs, jnp.uint32).reshape(E * D // (2 * LANES), LANES)


def _gelu_exact(x):
    return 0.5 * x * (1.0 + lax.erf(x * (2.0 ** -0.5)))


def _fold(a, b, s, sub):
    ta = a + pltpu.roll(a, s, 0)
    tb = b + pltpu.roll(b, SUBLANES - s, 0)
    return jnp.where((sub & s) != 0, ta, tb)


def _fold_first(vs, sub):
    upper = (sub & 4) != 0
    return tuple(jnp.where(upper, vs[r + 4], vs[r]) + pltpu.roll(jnp.where(upper, vs[r], vs[r + 4]), 4, 0)
                 for r in range(4))


def _fold_rest(ms, sub):
    return _fold(_fold(ms[3], ms[1], 2, sub), _fold(ms[2], ms[0], 2, sub), 1, sub)


def _act_kernel(nba, nbv, eloc_hbm, st_hbm, h_ref, u_ref, e_ref, gate_ref, w_ref,
                e_smem, st_smem, stage_ref, act_ref, sem):
    i = pl.program_id(0)
    b = pl.program_id(1)
    Tk = h_ref.shape[0]
    T = Tk * pl.num_programs(0)
    n = Tk * PEER_PAIRS
    ns = Tk * PEER_ST_STRIDE
    lg = int(math.log2(nba))

    def copies_for(step, slot):
        si = lax.shift_right_logical(step, lg)
        sb = step & (nba - 1)
        src_e = eloc_hbm.at[pl.ds(pl.multiple_of((sb * T + si * Tk) * PEER_PAIRS, n), n)]
        src_s = st_hbm.at[pl.ds(pl.multiple_of(si * ns, ns), ns)]
        return (pltpu.make_async_copy(src_e, e_smem.at[pl.ds(pl.multiple_of(slot * n, n), n)], sem.at[slot, 0]),
                pltpu.make_async_copy(src_s, st_smem.at[pl.ds(pl.multiple_of(slot * ns, ns), ns)], sem.at[slot, 1]))

    slot = _prefetch_lists(i * nba + b, pl.num_programs(0) * nba, copies_for)
    eoff = slot * n
    soff = slot * ns + b
    sub = lax.broadcasted_iota(i32, (SUBLANES, LANES), 0)

    def token(t, carry):
        s0, s1, cs, ce = _chunk_range(st_smem, soff + t * PEER_ST_STRIDE, PEER_CHUNK)
        hrow = h_ref[t]
        base = eoff + t * PEER_PAIRS

        def flush(c, folded):
            first = pl.multiple_of(c * PEER_CHUNK, PEER_CHUNK)
            part = _fold_rest(folded, sub)
            pos = sub + first
            pltpu.store(stage_ref.at[t, pl.ds(first, PEER_CHUNK), :], part,
                        mask=(pos >= s0) & (pos < s1))

        def expert_ids(c):
            first = base + jnp.minimum(c, PEER_NCHUNK - 1) * PEER_CHUNK
            return tuple(e_smem[first + k] for k in range(PEER_CHUNK))

        def chunk(c, pending):
            c_prev, folded_prev, ids = pending
            halves = []
            for k in range(PEER_CHUNK):
                pr = hrow * _table_row(u_ref, ids[k])
                halves.append(pr[:SUBLANES] + pr[SUBLANES:])
            flush(c_prev, folded_prev)
            return c, _fold_first(halves, sub), expert_ids(c + 1)

        zero = jnp.zeros((SUBLANES, LANES), f32)
        pending = (jnp.minimum(cs, PEER_NCHUNK - 1), (zero,) * (PEER_CHUNK // 2), expert_ids(cs))
        c_last, folded_last, _ = lax.fori_loop(cs, ce, chunk, pending)
        flush(c_last, folded_last)
        return carry

    lax.fori_loop(0, Tk, token, 0)

    @pl.when(b == nba - 1)
    def _():
        unroll = 4

        def reduce_tokens(g, carry):
            for k in range(unroll):
                t = g * unroll + k
                act_ref[pl.ds(t, 1), :] = jnp.sum(stage_ref[t].T, axis=0, keepdims=True)
            return carry

        lax.fori_loop(0, Tk // unroll, reduce_tokens, 0)
        w = gate_ref[...] * _gelu_exact(act_ref[...])
        blk = lax.shift_right_logical(e_ref[...], int(math.log2(PEER_NKEYS * PEER_NKEYS // nbv)))
        for bb in range(nbv):
            w_ref[bb] = jnp.where(blk == bb, w, 0.0)


def _peer_act(eloc, starts, h3, u_rows, e_tok, gate, nba, nbv, tk):
    T = h3.shape[0]
    blk_rows = u_rows.shape[0] // nba
    rowblk = h3.shape[1:]
    return pl.pallas_call(
        functools.partial(_act_kernel, nba, nbv),
        grid=(T // tk, nba),
        in_specs=[pl.BlockSpec(memory_space=pl.ANY),
                  pl.BlockSpec(memory_space=pl.ANY),
                  pl.BlockSpec((tk,) + rowblk, lambda i, b: (i, 0, 0)),
                  pl.BlockSpec((blk_rows, LANES), lambda i, b: (b, 0)),
                  pl.BlockSpec((tk, PEER_PAIRS), lambda i, b: (i, 0)),
                  pl.BlockSpec((tk, PEER_PAIRS), lambda i, b: (i, 0))],
        out_specs=pl.BlockSpec((nbv, tk, PEER_PAIRS), lambda i, b: (0, i, 0)),
        out_shape=jax.ShapeDtypeStruct((nbv, T, PEER_PAIRS), f32),
        scratch_shapes=[pltpu.SMEM((2 * tk * PEER_PAIRS,), i32),
                        pltpu.SMEM((2 * tk * PEER_ST_STRIDE,), i32),
                        pltpu.VMEM((tk, PEER_PAIRS, LANES), f32),
                        pltpu.VMEM((tk, PEER_PAIRS), f32),
                        pltpu.SemaphoreType.DMA((2, 2))],
        compiler_params=_cparams("arbitrary", "arbitrary"),
    )(eloc, starts, h3, u_rows, e_tok, gate)


PEER_VAL_CHUNK = 4 * PEER_CHUNK
PEER_VAL_NACC = 4


def _val_kernel(eloc_hbm, st_hbm, w_hbm, v_hbm, o_ref, e_smem, st_smem, w_smem, v_ref, sem, vsem):
    b = pl.program_id(0)
    i = pl.program_id(1)
    ni = pl.num_programs(1)
    Tk = o_ref.shape[1]
    T = Tk * ni
    n = Tk * PEER_PAIRS
    ns = Tk * PEER_ST_STRIDE
    blk_rows = v_ref.shape[0]

    @pl.when(i == 0)
    def _():
        cp = pltpu.make_async_copy(v_hbm.at[pl.ds(b * blk_rows, blk_rows)], v_ref, vsem)
        cp.start()
        cp.wait()

    def copies_for(step, slot):
        sb = step // ni
        si = step - sb * ni
        lists = pl.ds(pl.multiple_of((sb * T + si * Tk) * PEER_PAIRS, n), n)
        dst = pl.ds(pl.multiple_of(slot * n, n), n)
        return (pltpu.make_async_copy(eloc_hbm.at[lists], e_smem.at[dst], sem.at[slot, 0]),
                pltpu.make_async_copy(st_hbm.at[pl.ds(pl.multiple_of(si * ns, ns), ns)],
                                      st_smem.at[pl.ds(pl.multiple_of(slot * ns, ns), ns)], sem.at[slot, 1]),
                pltpu.make_async_copy(w_hbm.at[lists], w_smem.at[dst], sem.at[slot, 2]))

    slot = _prefetch_lists(b * ni + i, pl.num_programs(0) * ni, copies_for)
    eoff = slot * n
    soff = slot * ns + b

    def token(t, carry):
        _, _, cs, ce = _chunk_range(st_smem, soff + t * PEER_ST_STRIDE, PEER_VAL_CHUNK)
        base = eoff + t * PEER_PAIRS

        def chunk(c, accs):
            first = base + c * PEER_VAL_CHUNK
            accs = list(accs)
            for k in range(PEER_VAL_CHUNK):
                term = w_smem[first + k] * _table_row(v_ref, e_smem[first + k])
                accs[k % PEER_VAL_NACC] = accs[k % PEER_VAL_NACC] + term
            return tuple(accs)

        zero = jnp.zeros(o_ref.shape[2:], f32)
        accs = lax.fori_loop(cs, ce, chunk, (zero,) * PEER_VAL_NACC)
        o_ref[0, t] = (accs[0] + accs[1]) + (accs[2] + accs[3])
        return carry

    lax.fori_loop(0, Tk, token, 0)


def _peer_val(eloc, starts, w, v_rows, nb, tk):
    T = eloc.shape[0] // (nb * PEER_PAIRS)
    blk_rows = v_rows.shape[0] // nb
    rowblk = (2 * PEER_TABLE_ROWS, LANES)
    return pl.pallas_call(
        _val_kernel,
        grid=(nb, T // tk),
        in_specs=[pl.BlockSpec(memory_space=pl.ANY),
                  pl.BlockSpec(memory_space=pl.ANY),
                  pl.BlockSpec(memory_space=pl.ANY),
                  pl.BlockSpec(memory_space=pl.ANY)],
        out_specs=pl.BlockSpec((1, tk) + rowblk, lambda b, i: (b, i, 0, 0)),
        out_shape=jax.ShapeDtypeStruct((nb, T) + rowblk, f32),
        scratch_shapes=[pltpu.SMEM((2 * tk * PEER_PAIRS,), i32),
                        pltpu.SMEM((2 * tk * PEER_ST_STRIDE,), i32),
                        pltpu.SMEM((2 * tk * PEER_PAIRS,), f32),
                        pltpu.VMEM((blk_rows, LANES), v_rows.dtype),
                        pltpu.SemaphoreType.DMA((2, 3)),
                        pltpu.SemaphoreType.DMA(())],
        compiler_params=_cparams("arbitrary", "arbitrary"),
    )(eloc, starts, w, v_rows)


def _ln2_kernel(alpha, h_ref, f_ref, g_ref, b_ref, o_ref):
    o_ref[...] = _layer_norm_rows(alpha * h_ref[...] + jnp.sum(f_ref[...], axis=0), g_ref[...], b_ref[...])


def _ln2(h2, ffn, g, b, alpha, tm=512):
    T, D = h2.shape
    return pl.pallas_call(
        functools.partial(_ln2_kernel, alpha),
        grid=(T // tm,),
        in_specs=[pl.BlockSpec((tm, D), lambda i: (i, 0)),
                  pl.BlockSpec((ffn.shape[0], tm, D), lambda i: (0, i, 0)),
                  pl.BlockSpec((1, D), lambda i: (0, 0)),
                  pl.BlockSpec((1, D), lambda i: (0, 0))],
        out_specs=pl.BlockSpec((tm, D), lambda i: (i, 0)),
        out_shape=jax.ShapeDtypeStruct((T, D), f32),
        compiler_params=_cparams("parallel"),
    )(h2, ffn, g, b)


PEER_ACT_NBLOCKS = 4
PEER_VAL_NBLOCKS = 2
PEER_ACT_TOKENS = 256
PEER_VAL_TOKENS = 256


def _layer(x, w_in, w_a2, b_a, gla_norm_g, conv_w, conv_b, conv_ln_g, conv_ln_b, w_out,
           ln1_g, ln1_b, w_q, sub_k1, sub_k2, u_tab, v_tab, ln2_g, ln2_b, alpha):
    B, L, D = x.shape
    T = B * L
    kw = GLA_HEADS * GLA_DK
    vw = GLA_HEADS * GLA_DV
    split_r = 2 * kw + 2 * vw
    split_a = split_r + GLA_GATE_RANK
    cw = conv_w.shape[1]
    x2 = x.reshape(T, D)
    w_main = jnp.concatenate([w_in[:, :split_r], w_in[:, split_a:]], axis=1).astype(bf16)
    w_a = jnp.pad(w_in[:, split_r:split_a], ((0, 0), (0, LANES - GLA_GATE_RANK))).astype(bf16)
    w_a2p = jnp.pad(w_a2, ((0, LANES - GLA_GATE_RANK), (0, 0)))
    conv_wp = jnp.pad(conv_w, ((0, CONV_HALO - CONV_KERNEL), (0, 0)))
    proj, a_lr = _inproj(x2, w_main, w_a)
    proj3 = proj.reshape(B, L, -1)
    gla_out = _gla(proj3, a_lr.reshape(B, L, LANES), w_a2p, b_a.reshape(1, kw),
                   gla_norm_g.reshape(1, vw))
    conv_out = _conv(proj3, conv_wp, conv_b.reshape(1, cw), conv_ln_g.reshape(1, cw),
                     conv_ln_b.reshape(1, cw), val_blk=split_r // cw)
    h2 = _outproj(gla_out.reshape(T, vw), conv_out.reshape(T, cw), x2,
                  w_out[:vw].astype(bf16), w_out[vw:].astype(bf16),
                  ln1_g.reshape(1, D), ln1_b.reshape(1, D), alpha)
    q = _qproj(h2, w_q.astype(bf16))
    nba, nbv = PEER_ACT_NBLOCKS, PEER_VAL_NBLOCKS
    e_tok, eloc_a, eloc_v, gate, starts = _route(q, sub_k1.astype(bf16), sub_k2.astype(bf16), nba, nbv)
    rows = D // LANES
    u_rows = _pack_table(u_tab)
    v_rows = _pack_table(v_tab)
    h3 = h2.reshape(T, rows, LANES)
    starts_a = starts[:, :PEER_ST_STRIDE].reshape(-1)
    starts_v = starts[:, PEER_ST_STRIDE:2 * PEER_ST_STRIDE].reshape(-1)
    w = _peer_act(eloc_a.reshape(-1), starts_a, h3, u_rows, e_tok, gate, nba, nbv, PEER_ACT_TOKENS)
    ffn = _peer_val(eloc_v.reshape(-1), starts_v, w.reshape(-1), v_rows, nbv, PEER_VAL_TOKENS)
    return _ln2(h2, ffn.reshape(nbv, T, D), ln2_g.reshape(1, D), ln2_b.reshape(1, D), alpha).reshape(B, L, D)


def kernel(x, w_in, w_a2, b_a, gla_norm_g, conv_w, conv_b, conv_ln_g, conv_ln_b, w_out, ln1_g, ln1_b, w_q, sub_k1, sub_k2, u_tab, v_tab, ln2_g, ln2_b):
    depth = w_in.shape[0]
    alpha = (2.0 * depth) ** 0.25
    for l in range(depth):
        x = _layer(x, w_in[l], w_a2[l], b_a[l], gla_norm_g[l], conv_w[l], conv_b[l], conv_ln_g[l],
                   conv_ln_b[l], w_out[l], ln1_g[l], ln1_b[l], w_q[l], sub_k1[l], sub_k2[l],
                   u_tab[l], v_tab[l], ln2_g[l], ln2_b[l], alpha)
    return x
```

```python
import functools
import math

import jax
import jax.numpy as jnp
from jax import lax
from jax.experimental import pallas as pl
from jax.experimental.pallas import tpu as pltpu

f32 = jnp.float32
bf16 = jnp.bfloat16
i32 = jnp.int32

LANES = 128
SUBLANES = 8
VMEM_LIMIT_BYTES = 60 * 1024 * 1024

GLA_HEADS = 4
GLA_DK = 128
GLA_DV = 256
GLA_GATE_RANK = 16
GLA_TAU = 16.0
GLA_CHUNK = 64
CONV_KERNEL = 31
CONV_HALO = 32
PEER_HEADS = 8
PEER_NKEYS = 128
PEER_TOPK = 16
PEER_PAIRS = PEER_HEADS * PEER_TOPK
PEER_CHUNK = SUBLANES
PEER_NCHUNK = PEER_PAIRS // PEER_CHUNK
PEER_TABLE_ROWS = SUBLANES
LN_EPS = 1e-5


def _cparams(*sem):
    return pltpu.CompilerParams(dimension_semantics=sem, vmem_limit_bytes=VMEM_LIMIT_BYTES)


def _layer_norm_rows(x, g, b=None):
    mu = jnp.mean(x, axis=-1, keepdims=True)
    xc = x - mu
    var = jnp.mean(xc * xc, axis=-1, keepdims=True)
    out = xc * lax.rsqrt(var + LN_EPS) * g
    if b is not None:
        out = out + b
    return out


def _silu(x):
    return x * jax.nn.sigmoid(x)


def _inproj_kernel(x_ref, w_ref, wa_ref, o_ref, a_ref, xb_ref):
    @pl.when(pl.program_id(1) == 0)
    def _():
        xb = x_ref[...].astype(bf16)
        xb_ref[...] = xb
        a_ref[...] = jnp.dot(xb, wa_ref[...], preferred_element_type=f32)

    o_ref[...] = jnp.dot(xb_ref[...], w_ref[...], preferred_element_type=f32)


def _inproj(x2, w_main, w_a, tm=512, tn=1024):
    T, D = x2.shape
    N = w_main.shape[1]
    return pl.pallas_call(
        _inproj_kernel,
        grid=(T // tm, N // tn),
        in_specs=[pl.BlockSpec((tm, D), lambda i, j: (i, 0)),
                  pl.BlockSpec((D, tn), lambda i, j: (0, j)),
                  pl.BlockSpec((D, LANES), lambda i, j: (0, 0))],
        out_specs=[pl.BlockSpec((tm, tn), lambda i, j: (i, j)),
                   pl.BlockSpec((tm, LANES), lambda i, j: (i, 0))],
        out_shape=[jax.ShapeDtypeStruct((T, N), f32), jax.ShapeDtypeStruct((T, LANES), f32)],
        scratch_shapes=[pltpu.VMEM((tm, D), bf16)],
        compiler_params=_cparams("parallel", "arbitrary"),
    )(x2, w_main, w_a)


def _gla_kernel(q_ref, k_ref, v_ref, r_ref, a_ref, wa2_ref, ba_ref, g_ref, o_ref, st_ref):
    C = GLA_CHUNK
    L = q_ref.shape[1]

    @pl.when(pl.program_id(2) == 0)
    def _():
        st_ref[...] = jnp.zeros_like(st_ref)

    row = lax.broadcasted_iota(i32, (C, C), 0)
    col = lax.broadcasted_iota(i32, (C, C), 1)
    causal = row >= col
    tril = causal.astype(f32)
    scale = GLA_DK ** -0.5
    DK, DV = GLA_DK, GLA_DV
    heads = q_ref.shape[2] // DK
    nt = (((1,), (1,)), ((), ()))

    def chunk(n, carry):
        rows = pl.ds(pl.multiple_of(n * C, C), C)
        a_lr = a_ref[0, rows, :]
        for hh in range(heads):
            kc = slice(hh * DK, (hh + 1) * DK)
            vc = slice(hh * DV, (hh + 1) * DV)
            z = jnp.dot(a_lr, wa2_ref[:, kc], preferred_element_type=f32,
                        precision=lax.Precision.HIGHEST) + ba_ref[:, kc]
            log_a = (jnp.minimum(z, 0.0) - jnp.log1p(jnp.exp(-jnp.abs(z)))) / GLA_TAU
            b = jnp.dot(tril, log_a, preferred_element_type=f32, precision=lax.Precision.HIGHEST)
            b_ref = b[C // 2:C // 2 + 1, :]
            b_last = b[C - 1:C, :]
            q = q_ref[0, rows, kc] * scale
            k = k_ref[0, rows, kc]
            v = v_ref[0, rows, vc].astype(bf16)
            q_in = (q * jnp.exp(b - b_ref)).astype(bf16)
            k_in = (k * jnp.exp(b_ref - b)).astype(bf16)
            scores = lax.dot_general(q_in, k_in, nt, preferred_element_type=f32)
            scores = jnp.where(causal, scores, 0.0)
            o = jnp.dot(scores.astype(bf16), v, preferred_element_type=f32)
            st = st_ref[hh]
            q_dec = (q * jnp.exp(b)).astype(bf16)
            o = o + lax.dot_general(q_dec, st.astype(bf16), nt, preferred_element_type=f32)
            k_dec = (k * jnp.exp(b_last - b)).astype(bf16)
            kv_t = lax.dot_general(v, k_dec, (((0,), (0,)), ((), ())), preferred_element_type=f32)
            st_ref[hh] = st * jnp.exp(b_last) + kv_t
            o = _layer_norm_rows(o, g_ref[:, vc])
            o_ref[0, rows, vc] = (o * _silu(r_ref[0, rows, vc])).astype(o_ref.dtype)
        return carry

    lax.fori_loop(0, L // C, chunk, 0)


def _gla(proj3, a3, w_a2p, b_a, gla_g, heads_per_step=4, tl=1024):
    B, L, _ = proj3.shape
    H, hs = GLA_HEADS, heads_per_step
    tl = min(tl, L)
    DK, DV = hs * GLA_DK, hs * GLA_DV
    kq = H // hs
    kv = 2 * kq * DK // DV
    kr = kv + H // hs
    return pl.pallas_call(
        _gla_kernel,
        grid=(B, H // hs, L // tl),
        in_specs=[pl.BlockSpec((1, tl, DK), lambda b, h, l: (b, l, h)),
                  pl.BlockSpec((1, tl, DK), lambda b, h, l: (b, l, kq + h)),
                  pl.BlockSpec((1, tl, DV), lambda b, h, l: (b, l, kv + h)),
                  pl.BlockSpec((1, tl, DV), lambda b, h, l: (b, l, kr + h)),
                  pl.BlockSpec((1, tl, LANES), lambda b, h, l: (b, l, 0)),
                  pl.BlockSpec((LANES, DK), lambda b, h, l: (0, h)),
                  pl.BlockSpec((1, DK), lambda b, h, l: (0, h)),
                  pl.BlockSpec((1, DV), lambda b, h, l: (0, h))],
        out_specs=pl.BlockSpec((1, tl, DV), lambda b, h, l: (b, l, h)),
        out_shape=jax.ShapeDtypeStruct((B, L, H * GLA_DV), bf16),
        scratch_shapes=[pltpu.VMEM((hs, GLA_DV, GLA_DK), f32)],
        compiler_params=_cparams("parallel", "parallel", "arbitrary"),
    )(proj3, proj3, proj3, proj3, a3, w_a2p, b_a, gla_g)


def _conv_kernel(val_ref, gate_ref, w_ref, cb_ref, g_ref, b_ref, o_ref, glu_ref, acc_ref):
    tl = val_ref.shape[1]
    W = val_ref.shape[2]
    rc = 64

    @pl.when(pl.program_id(1) == 0)
    def _():
        glu_ref[0:CONV_HALO, :] = jnp.zeros((CONV_HALO, W), f32)

    @pl.when(pl.program_id(1) != 0)
    def _():
        glu_ref[0:CONV_HALO, :] = glu_ref[tl:tl + CONV_HALO, :]

    glu_ref[CONV_HALO:CONV_HALO + tl, :] = val_ref[0] * jax.nn.sigmoid(gate_ref[0])
    lead = CONV_HALO - (CONV_KERNEL - 1)

    def strip(s, carry):
        cols = pl.ds(pl.multiple_of(s * LANES, LANES), LANES)
        wj = w_ref[:, cols]
        bias = cb_ref[:, cols]
        for r0 in range(0, tl, rc):
            acc = jnp.zeros((rc, LANES), f32) + bias
            for j in range(CONV_KERNEL):
                acc = acc + glu_ref[pl.ds(r0 + lead + j, rc), cols] * wj[j:j + 1, :]
            acc_ref[pl.ds(r0, rc), cols] = acc
        return carry

    lax.fori_loop(0, W // LANES, strip, 0)
    y = _layer_norm_rows(acc_ref[...], g_ref[...], b_ref[...])
    o_ref[0] = _silu(y).astype(o_ref.dtype)


def _conv(proj3, conv_wp, conv_b, ln_g, ln_b, val_blk, tl=256):
    B, L, _ = proj3.shape
    W = conv_wp.shape[1]
    return pl.pallas_call(
        _conv_kernel,
        grid=(B, L // tl),
        in_specs=[pl.BlockSpec((1, tl, W), lambda b, l: (b, l, val_blk)),
                  pl.BlockSpec((1, tl, W), lambda b, l: (b, l, val_blk + 1)),
                  pl.BlockSpec(conv_wp.shape, lambda b, l: (0, 0)),
                  pl.BlockSpec((1, W), lambda b, l: (0, 0)),
                  pl.BlockSpec((1, W), lambda b, l: (0, 0)),
                  pl.BlockSpec((1, W), lambda b, l: (0, 0))],
        out_specs=pl.BlockSpec((1, tl, W), lambda b, l: (b, l, 0)),
        out_shape=jax.ShapeDtypeStruct((B, L, W), bf16),
        scratch_shapes=[pltpu.VMEM((tl + CONV_HALO, W), f32), pltpu.VMEM((tl, W), f32)],
        compiler_params=_cparams("parallel", "arbitrary"),
    )(proj3, proj3, conv_wp, conv_b, ln_g, ln_b)


def _outproj_kernel(alpha, ga_ref, cv_ref, x_ref, w1_ref, w2_ref, g_ref, b_ref, h_ref):
    mix = jnp.dot(ga_ref[...], w1_ref[...], preferred_element_type=f32)
    mix = mix + jnp.dot(cv_ref[...], w2_ref[...], preferred_element_type=f32)
    h_ref[...] = _layer_norm_rows(alpha * x_ref[...] + mix, g_ref[...], b_ref[...])


def _outproj(gla_out, conv_out, x2, w_o1, w_o2, g, b, alpha, tm=256):
    T, D = x2.shape
    W1 = gla_out.shape[1]
    W2 = conv_out.shape[1]
    return pl.pallas_call(
        functools.partial(_outproj_kernel, alpha),
        grid=(T // tm,),
        in_specs=[pl.BlockSpec((tm, W1), lambda i: (i, 0)),
                  pl.BlockSpec((tm, W2), lambda i: (i, 0)),
                  pl.BlockSpec((tm, D), lambda i: (i, 0)),
                  pl.BlockSpec((W1, D), lambda i: (0, 0)),
                  pl.BlockSpec((W2, D), lambda i: (0, 0)),
                  pl.BlockSpec((1, D), lambda i: (0, 0)),
                  pl.BlockSpec((1, D), lambda i: (0, 0))],
        out_specs=pl.BlockSpec((tm, D), lambda i: (i, 0)),
        out_shape=jax.ShapeDtypeStruct((T, D), f32),
        compiler_params=_cparams("parallel"),
    )(gla_out, conv_out, x2, w_o1, w_o2, g, b)


def _qproj_kernel(h_ref, w_ref, q_ref):
    q_ref[...] = jnp.dot(h_ref[...].astype(bf16), w_ref[...],
                         preferred_element_type=f32).astype(q_ref.dtype)


def _qproj(h2, w_q, tm=256):
    T, D = h2.shape
    N = w_q.shape[1]
    return pl.pallas_call(
        _qproj_kernel,
        grid=(T // tm,),
        in_specs=[pl.BlockSpec((tm, D), lambda i: (i, 0)),
                  pl.BlockSpec((D, N), lambda i: (0, 0))],
        out_specs=pl.BlockSpec((tm, N), lambda i: (i, 0)),
        out_shape=jax.ShapeDtypeStruct((T, N), bf16),
        compiler_params=_cparams("parallel"),
    )(h2, w_q)


def _topk_rows(s, k):
    R = s.shape[0]
    iota = lax.broadcasted_iota(i32, s.shape, 0)
    vals, idxs = [], []
    for _ in range(k):
        m = jnp.max(s, axis=0, keepdims=True)
        idx = jnp.min(jnp.where(s == m, iota, R), axis=0, keepdims=True)
        vals.append(m)
        idxs.append(idx)
        s = jnp.where(iota == idx, -jnp.inf, s)
    return vals, idxs


def _staircase_candidates(s1t, i1, s2t, i2):
    K = PEER_TOPK
    s1c, i1c = jnp.concatenate(s1t, axis=0), jnp.concatenate(i1, axis=0)
    s2c, i2c = jnp.concatenate(s2t, axis=0), jnp.concatenate(i2, axis=0)
    sub = lax.broadcasted_iota(i32, (SUBLANES, s1c.shape[1]), 0)
    vals, idxs = [], []
    a = 0
    while K // (a + 1) > 1:
        n_b = K // (a + 1)
        for b0 in range(0, n_b, SUBLANES):
            v = s1t[a] + s2c[b0:b0 + SUBLANES]
            if n_b - b0 < SUBLANES:
                v = jnp.where(sub < n_b - b0, v, -jnp.inf)
            vals.append(v)
            idxs.append(i1[a] * PEER_NKEYS + i2c[b0:b0 + SUBLANES])
        a += 1
    assert K - a == SUBLANES
    vals.append(s1c[a:K] + s2t[0])
    idxs.append(i1c[a:K] * PEER_NKEYS + i2[0])
    return jnp.concatenate(vals, axis=0), jnp.concatenate(idxs, axis=0)


def _bitonic_sort_groups(keys, vals):
    ng = len(keys)
    n = ng * SUBLANES
    sub = lax.broadcasted_iota(i32, keys[0].shape, 0)
    k = 2
    while k <= n:
        j = k // 2
        while j >= 1:
            if j >= SUBLANES:
                gj = j // SUBLANES
                for g in range(ng):
                    if g & gj:
                        continue
                    p = g | gj
                    asc = ((g * SUBLANES) & k) == 0
                    swap = keys[g] > keys[p] if asc else keys[g] < keys[p]
                    kg = jnp.where(swap, keys[p], keys[g])
                    kp = jnp.where(swap, keys[g], keys[p])
                    vg = jnp.where(swap, vals[p], vals[g])
                    vp = jnp.where(swap, vals[g], vals[p])
                    keys[g], keys[p], vals[g], vals[p] = kg, kp, vg, vp
            else:
                low = (sub & j) == 0
                for g in range(ng):
                    x, y = keys[g], vals[g]
                    px = jnp.where(low, pltpu.roll(x, SUBLANES - j, 0), pltpu.roll(x, j, 0))
                    py = jnp.where(low, pltpu.roll(y, SUBLANES - j, 0), pltpu.roll(y, j, 0))
                    if k >= SUBLANES:
                        asc = ((g * SUBLANES) & k) == 0
                        take_min = low if asc else jnp.logical_not(low)
                    else:
                        take_min = low == ((sub & k) == 0)
                    sel = (take_min & (px < x)) | (jnp.logical_not(take_min) & (px > x))
                    keys[g] = jnp.where(sel, px, x)
                    vals[g] = jnp.where(sel, py, y)
            j //= 2
        k *= 2
    return keys, vals


def _route_kernel(nba, nbv, q_ref, k1_ref, k2_ref, e_ref, eloca_ref, elocv_ref, gate_ref, st_ref):
    Tt = q_ref.shape[0]
    half = q_ref.shape[1] // PEER_HEADS // 2
    nt = (((1,), (1,)), ((), ()))
    e_rows, g_rows = [], []
    for h in range(PEER_HEADS):
        q1 = q_ref[:, (2 * h) * half:(2 * h + 1) * half]
        q2 = q_ref[:, (2 * h + 1) * half:(2 * h + 2) * half]
        s1 = lax.dot_general(k1_ref[h], q1, nt, preferred_element_type=f32)
        s2 = lax.dot_general(k2_ref[h], q2, nt, preferred_element_type=f32)
        s1t, i1 = _topk_rows(s1, PEER_TOPK)
        s2t, i2 = _topk_rows(s2, PEER_TOPK)
        cand, cidx = _staircase_candidates(s1t, i1, s2t, i2)
        iota = lax.broadcasted_iota(i32, cand.shape, 0)
        sc, ex = [], []
        for _ in range(PEER_TOPK):
            m = jnp.max(cand, axis=0, keepdims=True)
            pos = jnp.min(jnp.where(cand == m, iota, cand.shape[0]), axis=0, keepdims=True)
            hit = iota == pos
            ex.append(jnp.sum(jnp.where(hit, cidx, 0), axis=0, keepdims=True))
            sc.append(m)
            cand = jnp.where(hit, -jnp.inf, cand)
        p = [jnp.exp(s - sc[0]) for s in sc]
        denom = p[0]
        for t in p[1:]:
            denom = denom + t
        e_rows += ex
        g_rows += [t / denom for t in p]
    ng = PEER_PAIRS // SUBLANES
    keys = [jnp.concatenate(e_rows[g * SUBLANES:(g + 1) * SUBLANES], axis=0) for g in range(ng)]
    vals = [jnp.concatenate(g_rows[g * SUBLANES:(g + 1) * SUBLANES], axis=0) for g in range(ng)]
    keys, vals = _bitonic_sort_groups(keys, vals)
    e_tok = jnp.concatenate(keys, axis=0).T
    e_ref[...] = e_tok
    gate_ref[...] = jnp.concatenate(vals, axis=0).T
    lane = lax.broadcasted_iota(i32, (Tt, LANES), 1)
    starts = jnp.zeros((Tt, LANES), i32)
    for nb, eloc_ref, off in ((nba, eloca_ref, 0), (nbv, elocv_ref, PEER_ST_STRIDE)):
        bsz = (PEER_NKEYS * PEER_NKEYS) // nb
        for b in range(nb):
            eloc_ref[b] = jnp.clip(e_tok - b * bsz, 0, bsz - 1) * PEER_TABLE_ROWS
        for b in range(1, nb + 1):
            cnt = jnp.sum((e_tok < b * bsz).astype(i32), axis=1, keepdims=True)
            starts = jnp.where(lane == off + b, cnt, starts)
    st_ref[...] = starts


def _route(q, k1, k2, nba, nbv, tt=128):
    T, QW = q.shape
    H, NK, half = k1.shape
    return pl.pallas_call(
        functools.partial(_route_kernel, nba, nbv),
        grid=(T // tt,),
        in_specs=[pl.BlockSpec((tt, QW), lambda i: (i, 0)),
                  pl.BlockSpec((H, NK, half), lambda i: (0, 0, 0)),
                  pl.BlockSpec((H, NK, half), lambda i: (0, 0, 0))],
        out_specs=[pl.BlockSpec((tt, PEER_PAIRS), lambda i: (i, 0)),
                   pl.BlockSpec((nba, tt, PEER_PAIRS), lambda i: (0, i, 0)),
                   pl.BlockSpec((nbv, tt, PEER_PAIRS), lambda i: (0, i, 0)),
                   pl.BlockSpec((tt, PEER_PAIRS), lambda i: (i, 0)),
                   pl.BlockSpec((tt, LANES), lambda i: (i, 0))],
        out_shape=[jax.ShapeDtypeStruct((T, PEER_PAIRS), i32),
                   jax.ShapeDtypeStruct((nba, T, PEER_PAIRS), i32),
                   jax.ShapeDtypeStruct((nbv, T, PEER_PAIRS), i32),
                   jax.ShapeDtypeStruct((T, PEER_PAIRS), f32),
                   jax.ShapeDtypeStruct((T, LANES), i32)],
        compiler_params=_cparams("parallel"),
    )(q, k1, k2)


PEER_ST_STRIDE = SUBLANES


def _chunk_range(st_smem, idx, chunk):
    s0 = st_smem[idx]
    s1 = st_smem[idx + 1]
    sh = int(math.log2(chunk))
    cs = lax.shift_right_logical(s0, sh)
    ce = jnp.where(s1 > s0, lax.shift_right_logical(s1 + (chunk - 1), sh), cs)
    return s0, s1, cs, ce


def _prefetch_lists(step, nsteps, copies_for):
    slot = step & 1

    @pl.when(step == 0)
    def _():
        for c in copies_for(step, slot):
            c.start()

    @pl.when(step + 1 < nsteps)
    def _():
        for c in copies_for(step + 1, 1 - slot):
            c.start()

    for c in copies_for(step, slot):
        c.wait()
    return slot


def _table_row(tab_ref, row):
    words = tab_ref[pl.ds(pl.multiple_of(row, PEER_TABLE_ROWS), PEER_TABLE_ROWS), :]
    return pltpu.bitcast(words, bf16).astype(f32)


def _pack_table(tab):
    E, D = tab.shape
    bits = lax.bitcast_convert_type(tab.astype(bf16), jnp.uint16).astype(jnp.uint32)
    words = bits[:, :D // 2] | (bits[:, D // 2:] << 16)
    return words.reshape(E * D // (2 * LANES), LANES)


def _tile_rows(x):
    lead = x.shape[:-1]
    n = len(lead)
    x = x.reshape(lead + (2, x.shape[-1] // (2 * LANES), LANES))
    return jnp.swapaxes(x, n, n + 1).reshape(lead + (-1, LANES))


def _untile_rows(x):
    lead = x.shape[:-2]
    n = len(lead)
    x = x.reshape(lead + (x.shape[-2] // 2, 2, LANES))
    return jnp.swapaxes(x, n, n + 1).reshape(lead + (-1,))


def _gelu_exact(x):
    return 0.5 * x * (1.0 + lax.erf(x * (2.0 ** -0.5)))


def _fold(a, b, s, sub):
    ta = a + pltpu.roll(a, s, 0)
    tb = b + pltpu.roll(b, SUBLANES - s, 0)
    return jnp.where((sub & s) != 0, ta, tb)


def _fold_first(vs, sub):
    upper = (sub & 4) != 0
    return tuple(jnp.where(upper, vs[r + 4], vs[r]) + pltpu.roll(jnp.where(upper, vs[r], vs[r + 4]), 4, 0)
                 for r in range(4))


def _fold_rest(ms, sub):
    return _fold(_fold(ms[3], ms[1], 2, sub), _fold(ms[2], ms[0], 2, sub), 1, sub)


def _act_kernel(nba, nbv, eloc_hbm, st_hbm, h_ref, u_ref, e_ref, gate_ref, w_ref,
                e_smem, st_smem, stage_ref, act_ref, sem):
    i = pl.program_id(0)
    b = pl.program_id(1)
    Tk = h_ref.shape[0]
    T = Tk * pl.num_programs(0)
    n = Tk * PEER_PAIRS
    ns = Tk * PEER_ST_STRIDE
    lg = int(math.log2(nba))

    def copies_for(step, slot):
        si = lax.shift_right_logical(step, lg)
        sb = step & (nba - 1)
        src_e = eloc_hbm.at[pl.ds(pl.multiple_of((sb * T + si * Tk) * PEER_PAIRS, n), n)]
        src_s = st_hbm.at[pl.ds(pl.multiple_of(si * ns, ns), ns)]
        return (pltpu.make_async_copy(src_e, e_smem.at[pl.ds(pl.multiple_of(slot * n, n), n)], sem.at[slot, 0]),
                pltpu.make_async_copy(src_s, st_smem.at[pl.ds(pl.multiple_of(slot * ns, ns), ns)], sem.at[slot, 1]))

    slot = _prefetch_lists(i * nba + b, pl.num_programs(0) * nba, copies_for)
    eoff = slot * n
    soff = slot * ns + b
    sub = lax.broadcasted_iota(i32, (SUBLANES, LANES), 0)

    def token(t, carry):
        s0, s1, cs, ce = _chunk_range(st_smem, soff + t * PEER_ST_STRIDE, PEER_CHUNK)
        hrow = h_ref[t]
        base = eoff + t * PEER_PAIRS

        def flush(c, folded):
            first = pl.multiple_of(c * PEER_CHUNK, PEER_CHUNK)
            part = _fold_rest(folded, sub)
            pos = sub + first
            pltpu.store(stage_ref.at[t, pl.ds(first, PEER_CHUNK), :], part,
                        mask=(pos >= s0) & (pos < s1))

        def expert_ids(c):
            first = base + jnp.minimum(c, PEER_NCHUNK - 1) * PEER_CHUNK
            return tuple(e_smem[first + k] for k in range(PEER_CHUNK))

        def chunk(c, pending):
            c_prev, folded_prev, ids = pending
            halves = []
            for k in range(PEER_CHUNK):
                pr = hrow * _table_row(u_ref, ids[k])
                halves.append(pr[:SUBLANES] + pr[SUBLANES:])
            flush(c_prev, folded_prev)
            return c, _fold_first(halves, sub), expert_ids(c + 1)

        zero = jnp.zeros((SUBLANES, LANES), f32)
        pending = (jnp.minimum(cs, PEER_NCHUNK - 1), (zero,) * (PEER_CHUNK // 2), expert_ids(cs))
        c_last, folded_last, _ = lax.fori_loop(cs, ce, chunk, pending)
        flush(c_last, folded_last)
        return carry

    lax.fori_loop(0, Tk, token, 0)

    @pl.when(b == nba - 1)
    def _():
        unroll = 4

        def reduce_tokens(g, carry):
            for k in range(unroll):
                t = g * unroll + k
                act_ref[pl.ds(t, 1), :] = jnp.sum(stage_ref[t].T, axis=0, keepdims=True)
            return carry

        lax.fori_loop(0, Tk // unroll, reduce_tokens, 0)
        w = gate_ref[...] * _gelu_exact(act_ref[...])
        blk = lax.shift_right_logical(e_ref[...], int(math.log2(PEER_NKEYS * PEER_NKEYS // nbv)))
        for bb in range(nbv):
            w_ref[bb] = jnp.where(blk == bb, w, 0.0)


def _peer_act(eloc, starts, h3, u_rows, e_tok, gate, nba, nbv, tk):
    T = h3.shape[0]
    blk_rows = u_rows.shape[0] // nba
    rowblk = h3.shape[1:]
    return pl.pallas_call(
        functools.partial(_act_kernel, nba, nbv),
        grid=(T // tk, nba),
        in_specs=[pl.BlockSpec(memory_space=pl.ANY),
                  pl.BlockSpec(memory_space=pl.ANY),
                  pl.BlockSpec((tk,) + rowblk, lambda i, b: (i, 0, 0)),
                  pl.BlockSpec((blk_rows, LANES), lambda i, b: (b, 0)),
                  pl.BlockSpec((tk, PEER_PAIRS), lambda i, b: (i, 0)),
                  pl.BlockSpec((tk, PEER_PAIRS), lambda i, b: (i, 0))],
        out_specs=pl.BlockSpec((nbv, tk, PEER_PAIRS), lambda i, b: (0, i, 0)),
        out_shape=jax.ShapeDtypeStruct((nbv, T, PEER_PAIRS), f32),
        scratch_shapes=[pltpu.SMEM((2 * tk * PEER_PAIRS,), i32),
                        pltpu.SMEM((2 * tk * PEER_ST_STRIDE,), i32),
                        pltpu.VMEM((tk, PEER_PAIRS, LANES), f32),
                        pltpu.VMEM((tk, PEER_PAIRS), f32),
                        pltpu.SemaphoreType.DMA((2, 2))],
        compiler_params=_cparams("arbitrary", "arbitrary"),
    )(eloc, starts, h3, u_rows, e_tok, gate)


PEER_VAL_CHUNK = 4 * PEER_CHUNK
PEER_VAL_NACC = 4


def _val_kernel(eloc_hbm, st_hbm, w_hbm, v_hbm, o_ref, e_smem, st_smem, w_smem, v_ref, sem, vsem):
    b = pl.program_id(0)
    i = pl.program_id(1)
    ni = pl.num_programs(1)
    Tk = o_ref.shape[1]
    T = Tk * ni
    n = Tk * PEER_PAIRS
    ns = Tk * PEER_ST_STRIDE
    blk_rows = v_ref.shape[0]

    @pl.when(i == 0)
    def _():
        cp = pltpu.make_async_copy(v_hbm.at[pl.ds(b * blk_rows, blk_rows)], v_ref, vsem)
        cp.start()
        cp.wait()

    def copies_for(step, slot):
        sb = step // ni
        si = step - sb * ni
        lists = pl.ds(pl.multiple_of((sb * T + si * Tk) * PEER_PAIRS, n), n)
        dst = pl.ds(pl.multiple_of(slot * n, n), n)
        return (pltpu.make_async_copy(eloc_hbm.at[lists], e_smem.at[dst], sem.at[slot, 0]),
                pltpu.make_async_copy(st_hbm.at[pl.ds(pl.multiple_of(si * ns, ns), ns)],
                                      st_smem.at[pl.ds(pl.multiple_of(slot * ns, ns), ns)], sem.at[slot, 1]),
                pltpu.make_async_copy(w_hbm.at[lists], w_smem.at[dst], sem.at[slot, 2]))

    slot = _prefetch_lists(b * ni + i, pl.num_programs(0) * ni, copies_for)
    eoff = slot * n
    soff = slot * ns + b

    def token(t, carry):
        _, _, cs, ce = _chunk_range(st_smem, soff + t * PEER_ST_STRIDE, PEER_VAL_CHUNK)
        base = eoff + t * PEER_PAIRS

        def chunk(c, accs):
            first = base + c * PEER_VAL_CHUNK
            accs = list(accs)
            for k in range(PEER_VAL_CHUNK):
                term = w_smem[first + k] * _table_row(v_ref, e_smem[first + k])
                accs[k % PEER_VAL_NACC] = accs[k % PEER_VAL_NACC] + term
            return tuple(accs)

        zero = jnp.zeros(o_ref.shape[2:], f32)
        accs = lax.fori_loop(cs, ce, chunk, (zero,) * PEER_VAL_NACC)
        o_ref[0, t] = (accs[0] + accs[1]) + (accs[2] + accs[3])
        return carry

    lax.fori_loop(0, Tk, token, 0)


def _peer_val(eloc, starts, w, v_rows, nb, tk):
    T = eloc.shape[0] // (nb * PEER_PAIRS)
    blk_rows = v_rows.shape[0] // nb
    rowblk = (2 * PEER_TABLE_ROWS, LANES)
    return pl.pallas_call(
        _val_kernel,
        grid=(nb, T // tk),
        in_specs=[pl.BlockSpec(memory_space=pl.ANY),
                  pl.BlockSpec(memory_space=pl.ANY),
                  pl.BlockSpec(memory_space=pl.ANY),
                  pl.BlockSpec(memory_space=pl.ANY)],
        out_specs=pl.BlockSpec((1, tk) + rowblk, lambda b, i: (b, i, 0, 0)),
        out_shape=jax.ShapeDtypeStruct((nb, T) + rowblk, f32),
        scratch_shapes=[pltpu.SMEM((2 * tk * PEER_PAIRS,), i32),
                        pltpu.SMEM((2 * tk * PEER_ST_STRIDE,), i32),
                        pltpu.SMEM((2 * tk * PEER_PAIRS,), f32),
                        pltpu.VMEM((blk_rows, LANES), v_rows.dtype),
                        pltpu.SemaphoreType.DMA((2, 3)),
                        pltpu.SemaphoreType.DMA(())],
        compiler_params=_cparams("arbitrary", "arbitrary"),
    )(eloc, starts, w, v_rows)


def _ln2_kernel(alpha, h_ref, f_ref, g_ref, b_ref, o_ref):
    o_ref[...] = _layer_norm_rows(alpha * h_ref[...] + jnp.sum(f_ref[...], axis=0), g_ref[...], b_ref[...])


def _ln2(h2, ffn, g, b, alpha, tm=512):
    T, D = h2.shape
    return pl.pallas_call(
        functools.partial(_ln2_kernel, alpha),
        grid=(T // tm,),
        in_specs=[pl.BlockSpec((tm, D), lambda i: (i, 0)),
                  pl.BlockSpec((ffn.shape[0], tm, D), lambda i: (0, i, 0)),
                  pl.BlockSpec((1, D), lambda i: (0, 0)),
                  pl.BlockSpec((1, D), lambda i: (0, 0))],
        out_specs=pl.BlockSpec((tm, D), lambda i: (i, 0)),
        out_shape=jax.ShapeDtypeStruct((T, D), f32),
        compiler_params=_cparams("parallel"),
    )(h2, ffn, g, b)


PEER_ACT_NBLOCKS = 4
PEER_VAL_NBLOCKS = 2
PEER_ACT_TOKENS = 256
PEER_VAL_TOKENS = 256


def _layer(x, w_in, w_a2, b_a, gla_norm_g, conv_w, conv_b, conv_ln_g, conv_ln_b, w_out,
           ln1_g, ln1_b, w_q, sub_k1, sub_k2, u_tab, v_tab, ln2_g, ln2_b, alpha):
    B, L, D = x.shape
    T = B * L
    kw = GLA_HEADS * GLA_DK
    vw = GLA_HEADS * GLA_DV
    split_r = 2 * kw + 2 * vw
    split_a = split_r + GLA_GATE_RANK
    cw = conv_w.shape[1]
    x2 = x.reshape(T, D)
    w_main = jnp.concatenate([w_in[:, :split_r], w_in[:, split_a:]], axis=1).astype(bf16)
    w_a = jnp.pad(w_in[:, split_r:split_a], ((0, 0), (0, LANES - GLA_GATE_RANK))).astype(bf16)
    w_a2p = jnp.pad(w_a2, ((0, LANES - GLA_GATE_RANK), (0, 0)))
    conv_wp = jnp.pad(conv_w, ((0, CONV_HALO - CONV_KERNEL), (0, 0)))
    proj, a_lr = _inproj(x2, w_main, w_a)
    proj3 = proj.reshape(B, L, -1)
    gla_out = _gla(proj3, a_lr.reshape(B, L, LANES), w_a2p, b_a.reshape(1, kw),
                   gla_norm_g.reshape(1, vw))
    conv_out = _conv(proj3, conv_wp, conv_b.reshape(1, cw), conv_ln_g.reshape(1, cw),
                     conv_ln_b.reshape(1, cw), val_blk=split_r // cw)
    h2 = _outproj(gla_out.reshape(T, vw), conv_out.reshape(T, cw), x2,
                  w_out[:vw].astype(bf16), w_out[vw:].astype(bf16),
                  ln1_g.reshape(1, D), ln1_b.reshape(1, D), alpha)
    q = _qproj(h2, w_q.astype(bf16))
    nba, nbv = PEER_ACT_NBLOCKS, PEER_VAL_NBLOCKS
    e_tok, eloc_a, eloc_v, gate, starts = _route(q, sub_k1.astype(bf16), sub_k2.astype(bf16), nba, nbv)
    u_rows = _pack_table(u_tab)
    v_rows = _pack_table(v_tab)
    h3 = _tile_rows(h2)
    starts_a = starts[:, :PEER_ST_STRIDE].reshape(-1)
    starts_v = starts[:, PEER_ST_STRIDE:2 * PEER_ST_STRIDE].reshape(-1)
    w = _peer_act(eloc_a.reshape(-1), starts_a, h3, u_rows, e_tok, gate, nba, nbv, PEER_ACT_TOKENS)
    ffn = _peer_val(eloc_v.reshape(-1), starts_v, w.reshape(-1), v_rows, nbv, PEER_VAL_TOKENS)
    return _ln2(h2, _untile_rows(ffn), ln2_g.reshape(1, D), ln2_b.reshape(1, D), alpha).reshape(B, L, D)


def kernel(x, w_in, w_a2, b_a, gla_norm_g, conv_w, conv_b, conv_ln_g, conv_ln_b, w_out, ln1_g, ln1_b, w_q, sub_k1, sub_k2, u_tab, v_tab, ln2_g, ln2_b):
    depth = w_in.shape[0]
    alpha = (2.0 * depth) ** 0.25
    for l in range(depth):
        x = _layer(x, w_in[l], w_a2[l], b_a[l], gla_norm_g[l], conv_w[l], conv_b[l], conv_ln_g[l],
                   conv_ln_b[l], w_out[l], ln1_g[l], ln1_b[l], w_q[l], sub_k1[l], sub_k2[l],
                   u_tab[l], v_tab[l], ln2_g[l], ln2_b[l], alpha)
    return x
```

```python
import functools
import math

import jax
import jax.numpy as jnp
from jax import lax
from jax.experimental import pallas as pl
from jax.experimental.pallas import tpu as pltpu

f32 = jnp.float32
bf16 = jnp.bfloat16
i32 = jnp.int32

LANES = 128
SUBLANES = 8
VMEM_LIMIT_BYTES = 60 * 1024 * 1024

GLA_HEADS = 4
GLA_DK = 128
GLA_DV = 256
GLA_GATE_RANK = 16
GLA_TAU = 16.0
GLA_CHUNK = 64
CONV_KERNEL = 31
CONV_HALO = 32
PEER_HEADS = 8
PEER_NKEYS = 128
PEER_TOPK = 16
PEER_PAIRS = PEER_HEADS * PEER_TOPK
PEER_CHUNK = SUBLANES
PEER_NCHUNK = PEER_PAIRS // PEER_CHUNK
PEER_TABLE_ROWS = SUBLANES
LN_EPS = 1e-5


def _cparams(*sem):
    return pltpu.CompilerParams(dimension_semantics=sem, vmem_limit_bytes=VMEM_LIMIT_BYTES)


def _layer_norm_rows(x, g, b=None):
    mu = jnp.mean(x, axis=-1, keepdims=True)
    xc = x - mu
    var = jnp.mean(xc * xc, axis=-1, keepdims=True)
    out = xc * lax.rsqrt(var + LN_EPS) * g
    if b is not None:
        out = out + b
    return out


def _silu(x):
    return x * jax.nn.sigmoid(x)


def _inproj_kernel(x_ref, w_ref, wa_ref, o_ref, a_ref, xb_ref):
    @pl.when(pl.program_id(1) == 0)
    def _():
        xb = x_ref[...].astype(bf16)
        xb_ref[...] = xb
        a_ref[...] = jnp.dot(xb, wa_ref[...], preferred_element_type=f32)

    o_ref[...] = jnp.dot(xb_ref[...], w_ref[...], preferred_element_type=f32)


def _inproj(x2, w_main, w_a, tm=512, tn=1024):
    T, D = x2.shape
    N = w_main.shape[1]
    return pl.pallas_call(
        _inproj_kernel,
        grid=(T // tm, N // tn),
        in_specs=[pl.BlockSpec((tm, D), lambda i, j: (i, 0)),
                  pl.BlockSpec((D, tn), lambda i, j: (0, j)),
                  pl.BlockSpec((D, LANES), lambda i, j: (0, 0))],
        out_specs=[pl.BlockSpec((tm, tn), lambda i, j: (i, j)),
                   pl.BlockSpec((tm, LANES), lambda i, j: (i, 0))],
        out_shape=[jax.ShapeDtypeStruct((T, N), f32), jax.ShapeDtypeStruct((T, LANES), f32)],
        scratch_shapes=[pltpu.VMEM((tm, D), bf16)],
        compiler_params=_cparams("parallel", "arbitrary"),
    )(x2, w_main, w_a)


def _gla_kernel(q_ref, k_ref, v_ref, r_ref, a_ref, wa2_ref, ba_ref, g_ref, o_ref, st_ref):
    C = GLA_CHUNK
    L = q_ref.shape[1]

    @pl.when(pl.program_id(2) == 0)
    def _():
        st_ref[...] = jnp.zeros_like(st_ref)

    row = lax.broadcasted_iota(i32, (C, C), 0)
    col = lax.broadcasted_iota(i32, (C, C), 1)
    causal = row >= col
    tril = causal.astype(f32)
    scale = GLA_DK ** -0.5
    DK, DV = GLA_DK, GLA_DV
    heads = q_ref.shape[2] // DK
    nt = (((1,), (1,)), ((), ()))

    def chunk(n, carry):
        rows = pl.ds(pl.multiple_of(n * C, C), C)
        a_lr = a_ref[0, rows, :]
        for hh in range(heads):
            kc = slice(hh * DK, (hh + 1) * DK)
            vc = slice(hh * DV, (hh + 1) * DV)
            z = jnp.dot(a_lr, wa2_ref[:, kc], preferred_element_type=f32,
                        precision=lax.Precision.HIGHEST) + ba_ref[:, kc]
            log_a = (jnp.minimum(z, 0.0) - jnp.log1p(jnp.exp(-jnp.abs(z)))) / GLA_TAU
            b = jnp.dot(tril, log_a, preferred_element_type=f32, precision=lax.Precision.HIGHEST)
            b_ref = b[C // 2:C // 2 + 1, :]
            b_last = b[C - 1:C, :]
            q = q_ref[0, rows, kc] * scale
            k = k_ref[0, rows, kc]
            v = v_ref[0, rows, vc].astype(bf16)
            q_in = (q * jnp.exp(b - b_ref)).astype(bf16)
            k_in = (k * jnp.exp(b_ref - b)).astype(bf16)
            scores = lax.dot_general(q_in, k_in, nt, preferred_element_type=f32)
            scores = jnp.where(causal, scores, 0.0)
            o = jnp.dot(scores.astype(bf16), v, preferred_element_type=f32)
            st = st_ref[hh]
            q_dec = (q * jnp.exp(b)).astype(bf16)
            o = o + lax.dot_general(q_dec, st.astype(bf16), nt, preferred_element_type=f32)
            k_dec = (k * jnp.exp(b_last - b)).astype(bf16)
            kv_t = lax.dot_general(v, k_dec, (((0,), (0,)), ((), ())), preferred_element_type=f32)
            st_ref[hh] = st * jnp.exp(b_last) + kv_t
            o = _layer_norm_rows(o, g_ref[:, vc])
            o_ref[0, rows, vc] = (o * _silu(r_ref[0, rows, vc])).astype(o_ref.dtype)
        return carry

    lax.fori_loop(0, L // C, chunk, 0)


def _gla(proj3, a3, w_a2p, b_a, gla_g, heads_per_step=4, tl=1024):
    B, L, _ = proj3.shape
    H, hs = GLA_HEADS, heads_per_step
    tl = min(tl, L)
    DK, DV = hs * GLA_DK, hs * GLA_DV
    kq = H // hs
    kv = 2 * kq * DK // DV
    kr = kv + H // hs
    return pl.pallas_call(
        _gla_kernel,
        grid=(B, H // hs, L // tl),
        in_specs=[pl.BlockSpec((1, tl, DK), lambda b, h, l: (b, l, h)),
                  pl.BlockSpec((1, tl, DK), lambda b, h, l: (b, l, kq + h)),
                  pl.BlockSpec((1, tl, DV), lambda b, h, l: (b, l, kv + h)),
                  pl.BlockSpec((1, tl, DV), lambda b, h, l: (b, l, kr + h)),
                  pl.BlockSpec((1, tl, LANES), lambda b, h, l: (b, l, 0)),
                  pl.BlockSpec((LANES, DK), lambda b, h, l: (0, h)),
                  pl.BlockSpec((1, DK), lambda b, h, l: (0, h)),
                  pl.BlockSpec((1, DV), lambda b, h, l: (0, h))],
        out_specs=pl.BlockSpec((1, tl, DV), lambda b, h, l: (b, l, h)),
        out_shape=jax.ShapeDtypeStruct((B, L, H * GLA_DV), bf16),
        scratch_shapes=[pltpu.VMEM((hs, GLA_DV, GLA_DK), f32)],
        compiler_params=_cparams("parallel", "parallel", "arbitrary"),
    )(proj3, proj3, proj3, proj3, a3, w_a2p, b_a, gla_g)


def _conv_kernel(val_ref, gate_ref, w_ref, cb_ref, g_ref, b_ref, o_ref, glu_ref, acc_ref):
    tl = val_ref.shape[1]
    W = val_ref.shape[2]
    rc = 64

    @pl.when(pl.program_id(1) == 0)
    def _():
        glu_ref[0:CONV_HALO, :] = jnp.zeros((CONV_HALO, W), f32)

    @pl.when(pl.program_id(1) != 0)
    def _():
        glu_ref[0:CONV_HALO, :] = glu_ref[tl:tl + CONV_HALO, :]

    glu_ref[CONV_HALO:CONV_HALO + tl, :] = val_ref[0] * jax.nn.sigmoid(gate_ref[0])
    lead = CONV_HALO - (CONV_KERNEL - 1)

    def strip(s, carry):
        cols = pl.ds(pl.multiple_of(s * LANES, LANES), LANES)
        wj = w_ref[:, cols]
        bias = cb_ref[:, cols]
        for r0 in range(0, tl, rc):
            acc = jnp.zeros((rc, LANES), f32) + bias
            for j in range(CONV_KERNEL):
                acc = acc + glu_ref[pl.ds(r0 + lead + j, rc), cols] * wj[j:j + 1, :]
            acc_ref[pl.ds(r0, rc), cols] = acc
        return carry

    lax.fori_loop(0, W // LANES, strip, 0)
    y = _layer_norm_rows(acc_ref[...], g_ref[...], b_ref[...])
    o_ref[0] = _silu(y).astype(o_ref.dtype)


def _conv(proj3, conv_wp, conv_b, ln_g, ln_b, val_blk, tl=256):
    B, L, _ = proj3.shape
    W = conv_wp.shape[1]
    return pl.pallas_call(
        _conv_kernel,
        grid=(B, L // tl),
        in_specs=[pl.BlockSpec((1, tl, W), lambda b, l: (b, l, val_blk)),
                  pl.BlockSpec((1, tl, W), lambda b, l: (b, l, val_blk + 1)),
                  pl.BlockSpec(conv_wp.shape, lambda b, l: (0, 0)),
                  pl.BlockSpec((1, W), lambda b, l: (0, 0)),
                  pl.BlockSpec((1, W), lambda b, l: (0, 0)),
                  pl.BlockSpec((1, W), lambda b, l: (0, 0))],
        out_specs=pl.BlockSpec((1, tl, W), lambda b, l: (b, l, 0)),
        out_shape=jax.ShapeDtypeStruct((B, L, W), bf16),
        scratch_shapes=[pltpu.VMEM((tl + CONV_HALO, W), f32), pltpu.VMEM((tl, W), f32)],
        compiler_params=_cparams("parallel", "arbitrary"),
    )(proj3, proj3, conv_wp, conv_b, ln_g, ln_b)


def _outproj_kernel(alpha, ga_ref, cv_ref, x_ref, w1_ref, w2_ref, g_ref, b_ref, h_ref):
    mix = jnp.dot(ga_ref[...], w1_ref[...], preferred_element_type=f32)
    mix = mix + jnp.dot(cv_ref[...], w2_ref[...], preferred_element_type=f32)
    h_ref[...] = _layer_norm_rows(alpha * x_ref[...] + mix, g_ref[...], b_ref[...])


def _outproj(gla_out, conv_out, x2, w_o1, w_o2, g, b, alpha, tm=256):
    T, D = x2.shape
    W1 = gla_out.shape[1]
    W2 = conv_out.shape[1]
    return pl.pallas_call(
        functools.partial(_outproj_kernel, alpha),
        grid=(T // tm,),
        in_specs=[pl.BlockSpec((tm, W1), lambda i: (i, 0)),
                  pl.BlockSpec((tm, W2), lambda i: (i, 0)),
                  pl.BlockSpec((tm, D), lambda i: (i, 0)),
                  pl.BlockSpec((W1, D), lambda i: (0, 0)),
                  pl.BlockSpec((W2, D), lambda i: (0, 0)),
                  pl.BlockSpec((1, D), lambda i: (0, 0)),
                  pl.BlockSpec((1, D), lambda i: (0, 0))],
        out_specs=pl.BlockSpec((tm, D), lambda i: (i, 0)),
        out_shape=jax.ShapeDtypeStruct((T, D), f32),
        compiler_params=_cparams("parallel"),
    )(gla_out, conv_out, x2, w_o1, w_o2, g, b)


def _qproj_kernel(h_ref, w_ref, q_ref):
    q_ref[...] = jnp.dot(h_ref[...].astype(bf16), w_ref[...],
                         preferred_element_type=f32).astype(q_ref.dtype)


def _qproj(h2, w_q, tm=256):
    T, D = h2.shape
    N = w_q.shape[1]
    return pl.pallas_call(
        _qproj_kernel,
        grid=(T // tm,),
        in_specs=[pl.BlockSpec((tm, D), lambda i: (i, 0)),
                  pl.BlockSpec((D, N), lambda i: (0, 0))],
        out_specs=pl.BlockSpec((tm, N), lambda i: (i, 0)),
        out_shape=jax.ShapeDtypeStruct((T, N), bf16),
        compiler_params=_cparams("parallel"),
    )(h2, w_q)


def _topk_rows(s, k):
    R = s.shape[0]
    iota = lax.broadcasted_iota(i32, s.shape, 0)
    vals, idxs = [], []
    for _ in range(k):
        m = jnp.max(s, axis=0, keepdims=True)
        idx = jnp.min(jnp.where(s == m, iota, R), axis=0, keepdims=True)
        vals.append(m)
        idxs.append(idx)
        s = jnp.where(iota == idx, -jnp.inf, s)
    return vals, idxs


def _staircase_candidates(s1t, i1, s2t, i2):
    K = PEER_TOPK
    s1c, i1c = jnp.concatenate(s1t, axis=0), jnp.concatenate(i1, axis=0)
    s2c, i2c = jnp.concatenate(s2t, axis=0), jnp.concatenate(i2, axis=0)
    sub = lax.broadcasted_iota(i32, (SUBLANES, s1c.shape[1]), 0)
    vals, idxs = [], []
    a = 0
    while K // (a + 1) > 1:
        n_b = K // (a + 1)
        for b0 in range(0, n_b, SUBLANES):
            v = s1t[a] + s2c[b0:b0 + SUBLANES]
            if n_b - b0 < SUBLANES:
                v = jnp.where(sub < n_b - b0, v, -jnp.inf)
            vals.append(v)
            idxs.append(i1[a] * PEER_NKEYS + i2c[b0:b0 + SUBLANES])
        a += 1
    assert K - a == SUBLANES
    vals.append(s1c[a:K] + s2t[0])
    idxs.append(i1c[a:K] * PEER_NKEYS + i2[0])
    return jnp.concatenate(vals, axis=0), jnp.concatenate(idxs, axis=0)


def _bitonic_sort_groups(keys, vals):
    ng = len(keys)
    n = ng * SUBLANES
    sub = lax.broadcasted_iota(i32, keys[0].shape, 0)
    k = 2
    while k <= n:
        j = k // 2
        while j >= 1:
            if j >= SUBLANES:
                gj = j // SUBLANES
                for g in range(ng):
                    if g & gj:
                        continue
                    p = g | gj
                    asc = ((g * SUBLANES) & k) == 0
                    swap = keys[g] > keys[p] if asc else keys[g] < keys[p]
                    kg = jnp.where(swap, keys[p], keys[g])
                    kp = jnp.where(swap, keys[g], keys[p])
                    vg = jnp.where(swap, vals[p], vals[g])
                    vp = jnp.where(swap, vals[g], vals[p])
                    keys[g], keys[p], vals[g], vals[p] = kg, kp, vg, vp
            else:
                low = (sub & j) == 0
                for g in range(ng):
                    x, y = keys[g], vals[g]
                    px = jnp.where(low, pltpu.roll(x, SUBLANES - j, 0), pltpu.roll(x, j, 0))
                    py = jnp.where(low, pltpu.roll(y, SUBLANES - j, 0), pltpu.roll(y, j, 0))
                    if k >= SUBLANES:
                        asc = ((g * SUBLANES) & k) == 0
                        take_min = low if asc else jnp.logical_not(low)
                    else:
                        take_min = low == ((sub & k) == 0)
                    sel = (take_min & (px < x)) | (jnp.logical_not(take_min) & (px > x))
                    keys[g] = jnp.where(sel, px, x)
                    vals[g] = jnp.where(sel, py, y)
            j //= 2
        k *= 2
    return keys, vals


def _route_kernel(nba, nbv, q_ref, k1_ref, k2_ref, e_ref, eloca_ref, elocv_ref, gate_ref, st_ref):
    Tt = q_ref.shape[0]
    half = q_ref.shape[1] // PEER_HEADS // 2
    nt = (((1,), (1,)), ((), ()))
    e_rows, g_rows = [], []
    for h in range(PEER_HEADS):
        q1 = q_ref[:, (2 * h) * half:(2 * h + 1) * half]
        q2 = q_ref[:, (2 * h + 1) * half:(2 * h + 2) * half]
        s1 = lax.dot_general(k1_ref[h], q1, nt, preferred_element_type=f32)
        s2 = lax.dot_general(k2_ref[h], q2, nt, preferred_element_type=f32)
        s1t, i1 = _topk_rows(s1, PEER_TOPK)
        s2t, i2 = _topk_rows(s2, PEER_TOPK)
        cand, cidx = _staircase_candidates(s1t, i1, s2t, i2)
        iota = lax.broadcasted_iota(i32, cand.shape, 0)
        sc, ex = [], []
        for _ in range(PEER_TOPK):
            m = jnp.max(cand, axis=0, keepdims=True)
            pos = jnp.min(jnp.where(cand == m, iota, cand.shape[0]), axis=0, keepdims=True)
            hit = iota == pos
            ex.append(jnp.sum(jnp.where(hit, cidx, 0), axis=0, keepdims=True))
            sc.append(m)
            cand = jnp.where(hit, -jnp.inf, cand)
        p = [jnp.exp(s - sc[0]) for s in sc]
        denom = p[0]
        for t in p[1:]:
            denom = denom + t
        e_rows += ex
        g_rows += [t / denom for t in p]
    ng = PEER_PAIRS // SUBLANES
    keys = [jnp.concatenate(e_rows[g * SUBLANES:(g + 1) * SUBLANES], axis=0) for g in range(ng)]
    vals = [jnp.concatenate(g_rows[g * SUBLANES:(g + 1) * SUBLANES], axis=0) for g in range(ng)]
    keys, vals = _bitonic_sort_groups(keys, vals)
    e_tok = jnp.concatenate(keys, axis=0).T
    e_ref[...] = e_tok
    gate_ref[...] = jnp.concatenate(vals, axis=0).T
    lane = lax.broadcasted_iota(i32, (Tt, LANES), 1)
    starts = jnp.zeros((Tt, LANES), i32)
    for nb, eloc_ref, off in ((nba, eloca_ref, 0), (nbv, elocv_ref, PEER_ST_STRIDE)):
        bsz = (PEER_NKEYS * PEER_NKEYS) // nb
        for b in range(nb):
            eloc_ref[b] = jnp.clip(e_tok - b * bsz, 0, bsz - 1) * PEER_TABLE_ROWS
        for b in range(1, nb + 1):
            cnt = jnp.sum((e_tok < b * bsz).astype(i32), axis=1, keepdims=True)
            starts = jnp.where(lane == off + b, cnt, starts)
    st_ref[...] = starts


def _route(q, k1, k2, nba, nbv, tt=128):
    T, QW = q.shape
    H, NK, half = k1.shape
    return pl.pallas_call(
        functools.partial(_route_kernel, nba, nbv),
        grid=(T // tt,),
        in_specs=[pl.BlockSpec((tt, QW), lambda i: (i, 0)),
                  pl.BlockSpec((H, NK, half), lambda i: (0, 0, 0)),
                  pl.BlockSpec((H, NK, half), lambda i: (0, 0, 0))],
        out_specs=[pl.BlockSpec((tt, PEER_PAIRS), lambda i: (i, 0)),
                   pl.BlockSpec((nba, tt, PEER_PAIRS), lambda i: (0, i, 0)),
                   pl.BlockSpec((nbv, tt, PEER_PAIRS), lambda i: (0, i, 0)),
                   pl.BlockSpec((tt, PEER_PAIRS), lambda i: (i, 0)),
                   pl.BlockSpec((tt, LANES), lambda i: (i, 0))],
        out_shape=[jax.ShapeDtypeStruct((T, PEER_PAIRS), i32),
                   jax.ShapeDtypeStruct((nba, T, PEER_PAIRS), i32),
                   jax.ShapeDtypeStruct((nbv, T, PEER_PAIRS), i32),
                   jax.ShapeDtypeStruct((T, PEER_PAIRS), f32),
                   jax.ShapeDtypeStruct((T, LANES), i32)],
        compiler_params=_cparams("parallel"),
    )(q, k1, k2)


PEER_ST_STRIDE = SUBLANES


def _chunk_range(st_smem, idx, chunk):
    s0 = st_smem[idx]
    s1 = st_smem[idx + 1]
    sh = int(math.log2(chunk))
    cs = lax.shift_right_logical(s0, sh)
    ce = jnp.where(s1 > s0, lax.shift_right_logical(s1 + (chunk - 1), sh), cs)
    return s0, s1, cs, ce


def _prefetch_lists(step, nsteps, copies_for):
    slot = step & 1

    @pl.when(step == 0)
    def _():
        for c in copies_for(step, slot):
            c.start()

    @pl.when(step + 1 < nsteps)
    def _():
        for c in copies_for(step + 1, 1 - slot):
            c.start()

    for c in copies_for(step, slot):
        c.wait()
    return slot


def _table_row(tab_ref, row):
    words = tab_ref[pl.ds(pl.multiple_of(row, PEER_TABLE_ROWS), PEER_TABLE_ROWS), :]
    return pltpu.bitcast(words, bf16).astype(f32)


def _pack_table(tab):
    E, D = tab.shape
    bits = lax.bitcast_convert_type(tab.astype(bf16), jnp.uint16).astype(jnp.uint32)
    words = bits[:, :D // 2] | (bits[:, D // 2:] << 16)
    return words.reshape(E * D // (2 * LANES), LANES)


def _tile_rows(x):
    lead = x.shape[:-1]
    n = len(lead)
    x = x.reshape(lead + (2, x.shape[-1] // (2 * LANES), LANES))
    return jnp.swapaxes(x, n, n + 1).reshape(lead + (-1, LANES))


def _untile_rows(x):
    lead = x.shape[:-2]
    n = len(lead)
    x = x.reshape(lead + (x.shape[-2] // 2, 2, LANES))
    return jnp.swapaxes(x, n, n + 1).reshape(lead + (-1,))


def _gelu_exact(x):
    return 0.5 * x * (1.0 + lax.erf(x * (2.0 ** -0.5)))


def _fold(a, b, s, sub):
    ta = a + pltpu.roll(a, s, 0)
    tb = b + pltpu.roll(b, SUBLANES - s, 0)
    return jnp.where((sub & s) != 0, ta, tb)


def _fold_first(vs, sub):
    upper = (sub & 4) != 0
    return tuple(jnp.where(upper, vs[r + 4], vs[r]) + pltpu.roll(jnp.where(upper, vs[r], vs[r + 4]), 4, 0)
                 for r in range(4))


def _fold_rest(ms, sub):
    return _fold(_fold(ms[3], ms[1], 2, sub), _fold(ms[2], ms[0], 2, sub), 1, sub)


def _act_kernel(nba, nbv, eloc_hbm, st_hbm, h_ref, u_ref, e_ref, gate_ref, w_ref,
                e_smem, st_smem, stage_ref, act_ref, sem):
    i = pl.program_id(0)
    b = pl.program_id(1)
    Tk = h_ref.shape[0]
    T = Tk * pl.num_programs(0)
    n = Tk * PEER_PAIRS
    ns = Tk * PEER_ST_STRIDE
    lg = int(math.log2(nba))

    def copies_for(step, slot):
        si = lax.shift_right_logical(step, lg)
        sb = step & (nba - 1)
        src_e = eloc_hbm.at[pl.ds(pl.multiple_of((sb * T + si * Tk) * PEER_PAIRS, n), n)]
        src_s = st_hbm.at[pl.ds(pl.multiple_of(si * ns, ns), ns)]
        return (pltpu.make_async_copy(src_e, e_smem.at[pl.ds(pl.multiple_of(slot * n, n), n)], sem.at[slot, 0]),
                pltpu.make_async_copy(src_s, st_smem.at[pl.ds(pl.multiple_of(slot * ns, ns), ns)], sem.at[slot, 1]))

    slot = _prefetch_lists(i * nba + b, pl.num_programs(0) * nba, copies_for)
    eoff = slot * n
    soff = slot * ns + b
    sub = lax.broadcasted_iota(i32, (SUBLANES, LANES), 0)

    def token(t, carry):
        s0, s1, cs, ce = _chunk_range(st_smem, soff + t * PEER_ST_STRIDE, PEER_CHUNK)
        hrow = h_ref[t]
        base = eoff + t * PEER_PAIRS

        def flush(c, folded):
            first = pl.multiple_of(c * PEER_CHUNK, PEER_CHUNK)
            part = _fold_rest(folded, sub)
            pos = sub + first
            pltpu.store(stage_ref.at[t, pl.ds(first, PEER_CHUNK), :], part,
                        mask=(pos >= s0) & (pos < s1))

        def expert_ids(c):
            first = base + jnp.minimum(c, PEER_NCHUNK - 1) * PEER_CHUNK
            return tuple(e_smem[first + k] for k in range(PEER_CHUNK))

        def chunk(c, pending):
            c_prev, folded_prev, ids = pending
            halves = []
            for k in range(PEER_CHUNK):
                pr = hrow * _table_row(u_ref, ids[k])
                halves.append(pr[:SUBLANES] + pr[SUBLANES:])
            flush(c_prev, folded_prev)
            return c, _fold_first(halves, sub), expert_ids(c + 1)

        zero = jnp.zeros((SUBLANES, LANES), f32)
        pending = (jnp.minimum(cs, PEER_NCHUNK - 1), (zero,) * (PEER_CHUNK // 2), expert_ids(cs))
        c_last, folded_last, _ = lax.fori_loop(cs, ce, chunk, pending)
        flush(c_last, folded_last)
        return carry

    lax.fori_loop(0, Tk, token, 0)

    @pl.when(b == nba - 1)
    def _():
        unroll = 4

        def reduce_tokens(g, carry):
            for k in range(unroll):
                t = g * unroll + k
                act_ref[pl.ds(t, 1), :] = jnp.sum(stage_ref[t].T, axis=0, keepdims=True)
            return carry

        lax.fori_loop(0, Tk // unroll, reduce_tokens, 0)
        w = gate_ref[...] * _gelu_exact(act_ref[...])
        blk = lax.shift_right_logical(e_ref[...], int(math.log2(PEER_NKEYS * PEER_NKEYS // nbv)))
        for bb in range(nbv):
            w_ref[bb] = jnp.where(blk == bb, w, 0.0)


def _peer_act(eloc, starts, h3, u_rows, e_tok, gate, nba, nbv, tk):
    T = h3.shape[0]
    blk_rows = u_rows.shape[0] // nba
    rowblk = h3.shape[1:]
    return pl.pallas_call(
        functools.partial(_act_kernel, nba, nbv),
        grid=(T // tk, nba),
        in_specs=[pl.BlockSpec(memory_space=pl.ANY),
                  pl.BlockSpec(memory_space=pl.ANY),
                  pl.BlockSpec((tk,) + rowblk, lambda i, b: (i, 0, 0)),
                  pl.BlockSpec((blk_rows, LANES), lambda i, b: (b, 0)),
                  pl.BlockSpec((tk, PEER_PAIRS), lambda i, b: (i, 0)),
                  pl.BlockSpec((tk, PEER_PAIRS), lambda i, b: (i, 0))],
        out_specs=pl.BlockSpec((nbv, tk, PEER_PAIRS), lambda i, b: (0, i, 0)),
        out_shape=jax.ShapeDtypeStruct((nbv, T, PEER_PAIRS), f32),
        scratch_shapes=[pltpu.SMEM((2 * tk * PEER_PAIRS,), i32),
                        pltpu.SMEM((2 * tk * PEER_ST_STRIDE,), i32),
                        pltpu.VMEM((tk, PEER_PAIRS, LANES), f32),
                        pltpu.VMEM((tk, PEER_PAIRS), f32),
                        pltpu.SemaphoreType.DMA((2, 2))],
        compiler_params=_cparams("arbitrary", "arbitrary"),
    )(eloc, starts, h3, u_rows, e_tok, gate)


PEER_VAL_CHUNK = 4 * PEER_CHUNK
PEER_VAL_NACC = 4


def _val_kernel(eloc_hbm, st_hbm, w_hbm, v_hbm, o_ref, e_smem, st_smem, w_smem, v_ref, sem, vsem):
    b = pl.program_id(0)
    i = pl.program_id(1)
    ni = pl.num_programs(1)
    Tk = o_ref.shape[1]
    T = Tk * ni
    n = Tk * PEER_PAIRS
    ns = Tk * PEER_ST_STRIDE
    blk_rows = v_ref.shape[0]

    @pl.when(i == 0)
    def _():
        cp = pltpu.make_async_copy(v_hbm.at[pl.ds(b * blk_rows, blk_rows)], v_ref, vsem)
        cp.start()
        cp.wait()

    def copies_for(step, slot):
        sb = step // ni
        si = step - sb * ni
        lists = pl.ds(pl.multiple_of((sb * T + si * Tk) * PEER_PAIRS, n), n)
        dst = pl.ds(pl.multiple_of(slot * n, n), n)
        return (pltpu.make_async_copy(eloc_hbm.at[lists], e_smem.at[dst], sem.at[slot, 0]),
                pltpu.make_async_copy(st_hbm.at[pl.ds(pl.multiple_of(si * ns, ns), ns)],
                                      st_smem.at[pl.ds(pl.multiple_of(slot * ns, ns), ns)], sem.at[slot, 1]),
                pltpu.make_async_copy(w_hbm.at[lists], w_smem.at[dst], sem.at[slot, 2]))

    slot = _prefetch_lists(b * ni + i, pl.num_programs(0) * ni, copies_for)
    eoff = slot * n
    soff = slot * ns + b

    def token(t, carry):
        _, _, cs, ce = _chunk_range(st_smem, soff + t * PEER_ST_STRIDE, PEER_VAL_CHUNK)
        base = eoff + t * PEER_PAIRS

        def chunk(c, accs):
            first = base + c * PEER_VAL_CHUNK
            accs = list(accs)
            for k in range(PEER_VAL_CHUNK):
                term = w_smem[first + k] * _table_row(v_ref, e_smem[first + k])
                accs[k % PEER_VAL_NACC] = accs[k % PEER_VAL_NACC] + term
            return tuple(accs)

        zero = jnp.zeros(o_ref.shape[2:], f32)
        accs = lax.fori_loop(cs, ce, chunk, (zero,) * PEER_VAL_NACC)
        o_ref[0, t] = (accs[0] + accs[1]) + (accs[2] + accs[3])
        return carry

    lax.fori_loop(0, Tk, token, 0)


def _peer_val(eloc, starts, w, v_rows, nb, tk):
    T = eloc.shape[0] // (nb * PEER_PAIRS)
    blk_rows = v_rows.shape[0] // nb
    rowblk = (2 * PEER_TABLE_ROWS, LANES)
    return pl.pallas_call(
        _val_kernel,
        grid=(nb, T // tk),
        in_specs=[pl.BlockSpec(memory_space=pl.ANY),
                  pl.BlockSpec(memory_space=pl.ANY),
                  pl.BlockSpec(memory_space=pl.ANY),
                  pl.BlockSpec(memory_space=pl.ANY)],
        out_specs=pl.BlockSpec((1, tk) + rowblk, lambda b, i: (b, i, 0, 0)),
        out_shape=jax.ShapeDtypeStruct((nb, T) + rowblk, f32),
        scratch_shapes=[pltpu.SMEM((2 * tk * PEER_PAIRS,), i32),
                        pltpu.SMEM((2 * tk * PEER_ST_STRIDE,), i32),
                        pltpu.SMEM((2 * tk * PEER_PAIRS,), f32),
                        pltpu.VMEM((blk_rows, LANES), v_rows.dtype),
                        pltpu.SemaphoreType.DMA((2, 3)),
                        pltpu.SemaphoreType.DMA(())],
        compiler_params=_cparams("arbitrary", "arbitrary"),
    )(eloc, starts, w, v_rows)


def _ln2_kernel(alpha, h_ref, f_ref, g_ref, b_ref, o_ref):
    x = alpha * h_ref[...] + jnp.sum(f_ref[...], axis=0)
    d = x.shape[1] * x.shape[2]
    mu = jnp.sum(jnp.sum(x, axis=2, keepdims=True), axis=1, keepdims=True) / d
    xc = x - mu
    var = jnp.sum(jnp.sum(xc * xc, axis=2, keepdims=True), axis=1, keepdims=True) / d
    o_ref[...] = xc * lax.rsqrt(var + LN_EPS) * g_ref[...] + b_ref[...]


def _ln2(h3, ffn, g, b, alpha, tm=256):
    T = h3.shape[0]
    tile = h3.shape[1:]
    return pl.pallas_call(
        functools.partial(_ln2_kernel, alpha),
        grid=(T // tm,),
        in_specs=[pl.BlockSpec((tm,) + tile, lambda i: (i, 0, 0)),
                  pl.BlockSpec((ffn.shape[0], tm) + tile, lambda i: (0, i, 0, 0)),
                  pl.BlockSpec((1,) + tile, lambda i: (0, 0, 0)),
                  pl.BlockSpec((1,) + tile, lambda i: (0, 0, 0))],
        out_specs=pl.BlockSpec((tm,) + tile, lambda i: (i, 0, 0)),
        out_shape=jax.ShapeDtypeStruct((T,) + tile, f32),
        compiler_params=_cparams("parallel"),
    )(h3, ffn, g, b)


PEER_ACT_NBLOCKS = 4
PEER_VAL_NBLOCKS = 2
PEER_ACT_TOKENS = 256
PEER_VAL_TOKENS = 256


def _layer(x, w_in, w_a2, b_a, gla_norm_g, conv_w, conv_b, conv_ln_g, conv_ln_b, w_out,
           ln1_g, ln1_b, w_q, sub_k1, sub_k2, u_tab, v_tab, ln2_g, ln2_b, alpha):
    B, L, D = x.shape
    T = B * L
    kw = GLA_HEADS * GLA_DK
    vw = GLA_HEADS * GLA_DV
    split_r = 2 * kw + 2 * vw
    split_a = split_r + GLA_GATE_RANK
    cw = conv_w.shape[1]
    x2 = x.reshape(T, D)
    w_main = jnp.concatenate([w_in[:, :split_r], w_in[:, split_a:]], axis=1).astype(bf16)
    w_a = jnp.pad(w_in[:, split_r:split_a], ((0, 0), (0, LANES - GLA_GATE_RANK))).astype(bf16)
    w_a2p = jnp.pad(w_a2, ((0, LANES - GLA_GATE_RANK), (0, 0)))
    conv_wp = jnp.pad(conv_w, ((0, CONV_HALO - CONV_KERNEL), (0, 0)))
    proj, a_lr = _inproj(x2, w_main, w_a)
    proj3 = proj.reshape(B, L, -1)
    gla_out = _gla(proj3, a_lr.reshape(B, L, LANES), w_a2p, b_a.reshape(1, kw),
                   gla_norm_g.reshape(1, vw))
    conv_out = _conv(proj3, conv_wp, conv_b.reshape(1, cw), conv_ln_g.reshape(1, cw),
                     conv_ln_b.reshape(1, cw), val_blk=split_r // cw)
    h2 = _outproj(gla_out.reshape(T, vw), conv_out.reshape(T, cw), x2,
                  w_out[:vw].astype(bf16), w_out[vw:].astype(bf16),
                  ln1_g.reshape(1, D), ln1_b.reshape(1, D), alpha)
    q = _qproj(h2, w_q.astype(bf16))
    nba, nbv = PEER_ACT_NBLOCKS, PEER_VAL_NBLOCKS
    e_tok, eloc_a, eloc_v, gate, starts = _route(q, sub_k1.astype(bf16), sub_k2.astype(bf16), nba, nbv)
    u_rows = _pack_table(u_tab)
    v_rows = _pack_table(v_tab)
    h3 = _tile_rows(h2)
    starts_a = starts[:, :PEER_ST_STRIDE].reshape(-1)
    starts_v = starts[:, PEER_ST_STRIDE:2 * PEER_ST_STRIDE].reshape(-1)
    w = _peer_act(eloc_a.reshape(-1), starts_a, h3, u_rows, e_tok, gate, nba, nbv, PEER_ACT_TOKENS)
    ffn = _peer_val(eloc_v.reshape(-1), starts_v, w.reshape(-1), v_rows, nbv, PEER_VAL_TOKENS)
    out = _ln2(h3, ffn, _tile_rows(ln2_g.reshape(1, D)), _tile_rows(ln2_b.reshape(1, D)), alpha)
    return _untile_rows(out).reshape(B, L, D)


def kernel(x, w_in, w_a2, b_a, gla_norm_g, conv_w, conv_b, conv_ln_g, conv_ln_b, w_out, ln1_g, ln1_b, w_q, sub_k1, sub_k2, u_tab, v_tab, ln2_g, ln2_b):
    depth = w_in.shape[0]
    alpha = (2.0 * depth) ** 0.25
    for l in range(depth):
        x = _layer(x, w_in[l], w_a2[l], b_a[l], gla_norm_g[l], conv_w[l], conv_b[l], conv_ln_g[l],
                   conv_ln_b[l], w_out[l], ln1_g[l], ln1_b[l], w_q[l], sub_k1[l], sub_k2[l],
                   u_tab[l], v_tab[l], ln2_g[l], ln2_b[l], alpha)
    return x
```
